```python
import jax, jax.numpy as jnp
from jax import lax
import numpy as np

D_MODEL = 2048
BATCH = 8
SEQ = 2048
DEPTH = 1
DEC_BATCH = 8
DEC_SEQ = 64
PAST_LEN = 1024

CHUNK = 64
Q_BLOCK = 128
FOX_HEADS = 12
FOX_HEAD_DIM = 128
FOX_WIDTH = FOX_HEADS * FOX_HEAD_DIM
CONV_WIDTH = 1536
CONV_K = 3
MEM_TOKENS = 256
MEM_HEADS = 4
MEM_HEAD_DIM = 256
MEM_WIDTH = MEM_HEADS * MEM_HEAD_DIM
N_BRANCH = 3
LN_EPS = 1e-5
NEG_INF = -1e30
DN_ALPHA = (2 * DEPTH) ** 0.25
DN_BETA = (8 * DEPTH) ** -0.25
IN_SIZES = (FOX_WIDTH, FOX_WIDTH, FOX_WIDTH, FOX_HEADS, FOX_WIDTH,
            CONV_WIDTH, CONV_WIDTH, CONV_WIDTH, CONV_WIDTH, MEM_WIDTH, MEM_WIDTH)
IN_WIDTH = 4 * FOX_WIDTH + FOX_HEADS + 4 * CONV_WIDTH + 2 * MEM_WIDTH

kernel_name = "fox_shortconv_memxattn_gated_hybrid_step"


def _layernorm(x, g, b):
    xf = x.astype(jnp.float32)
    mu = jnp.mean(xf, axis=-1, keepdims=True)
    var = jnp.mean(jnp.square(xf - mu), axis=-1, keepdims=True)
    y = (xf - mu) * lax.rsqrt(var + LN_EPS) * g.astype(jnp.float32) + b.astype(jnp.float32)
    return y.astype(x.dtype)


def _project(x, w_in, fox_bf):
    B, T, _ = x.shape
    z = jnp.einsum('btd,dn->btn', x, w_in)
    offs = tuple(int(o) for o in np.cumsum(IN_SIZES)[:-1])
    q, k, v, f, g_fox, b_gate, c_gate, h_in, g_conv, mq, g_mem = jnp.split(z, offs, axis=-1)
    q = q.reshape(B, T, FOX_HEADS, FOX_HEAD_DIM)
    k = k.reshape(B, T, FOX_HEADS, FOX_HEAD_DIM)
    v = v.reshape(B, T, FOX_HEADS, FOX_HEAD_DIM)
    logf = jax.nn.log_sigmoid(f.astype(jnp.float32) + fox_bf.astype(jnp.float32))
    u = c_gate * h_in
    mq = mq.reshape(B, T, MEM_HEADS, MEM_HEAD_DIM)
    return q, k, v, logf, g_fox, b_gate, u, g_conv, mq, g_mem


def _fox_prompt(q, k, v, logf):
    B, S, H, Dh = q.shape
    nb = S // Q_BLOCK
    scale = Dh ** -0.5
    cum = jnp.transpose(jnp.cumsum(logf, axis=1), (0, 2, 1))
    k_pos = jnp.arange(S)
    qb = jnp.moveaxis(q.reshape(B, nb, Q_BLOCK, H, Dh), 1, 0)
    cb = jnp.moveaxis(cum.reshape(B, H, nb, Q_BLOCK), 2, 0)
    pb = k_pos.reshape(nb, Q_BLOCK)

    def block(args):
        qi, ci, pi = args
        s = jnp.einsum('bqhd,bkhd->bhqk', qi, k).astype(jnp.float32) * scale
        s = s + ci[..., :, None] - cum[:, :, None, :]
        s = jnp.where(k_pos[None, :] <= pi[:, None], s, NEG_INF)
        p = jax.nn.softmax(s, axis=-1).astype(v.dtype)
        return jnp.einsum('bhqk,bkhd->bqhd', p, v)

    o = lax.map(block, (qb, cb, pb))
    return jnp.moveaxis(o, 0, 1).reshape(B, S, H * Dh)


def _fox_sample(q, k_new, v_new, logf_new, k_cache, v_cache, logf_cache):
    B, T, H, Dh = q.shape
    P = k_cache.shape[1]
    scale = Dh ** -0.5
    k = jnp.concatenate([k_cache.astype(k_new.dtype), k_new], axis=1)
    v = jnp.concatenate([v_cache.astype(v_new.dtype), v_new], axis=1)
    logf = jnp.concatenate([logf_cache.astype(jnp.float32), logf_new], axis=1)
    cum = jnp.transpose(jnp.cumsum(logf, axis=1), (0, 2, 1))
    s = jnp.einsum('bqhd,bkhd->bhqk', q, k).astype(jnp.float32) * scale
    s = s + cum[:, :, P:, None] - cum[:, :, None, :]
    mask = jnp.arange(P + T)[None, :] <= (P + jnp.arange(T))[:, None]
    s = jnp.where(mask, s, NEG_INF)
    p = jax.nn.softmax(s, axis=-1).astype(v.dtype)
    return jnp.einsum('bhqk,bkhd->bqhd', p, v).reshape(B, T, H * Dh)


def _short_conv(u, prev, w, b):
    T = u.shape[1]
    up = jnp.concatenate([prev.astype(u.dtype), u], axis=1)
    y = sum(w[j] * up[:, j:j + T] for j in range(CONV_K)) + b
    return y, up[:, T:]


def _mem_kv(mem, w_mem_kv):
    B, M, _ = mem.shape
    kv = jnp.einsum('bmd,dn->bmn', mem, w_mem_kv)
    mk, mv = jnp.split(kv, 2, axis=-1)
    return mk.reshape(B, M, MEM_HEADS, MEM_HEAD_DIM), mv.reshape(B, M, MEM_HEADS, MEM_HEAD_DIM)


def _mem_attn(q, mk, mv):
    B, T, _, _ = q.shape
    s = jnp.einsum('bthd,bmhd->bhtm', q, mk.astype(q.dtype)).astype(jnp.float32) * (MEM_HEAD_DIM ** -0.5)
    p = jax.nn.softmax(s, axis=-1).astype(q.dtype)
    return jnp.einsum('bhtm,bmhd->bthd', p, mv.astype(q.dtype)).reshape(B, T, MEM_WIDTH)


def _combine(x, o_fox, g_fox, o_conv, g_conv, o_mem, g_mem,
             w_fox_out, w_conv_out, w_mem_out, w_merge, b_merge, w_o, ln_g, ln_b):
    y_fox = jnp.einsum('btc,cd->btd', o_fox * jax.nn.silu(g_fox), w_fox_out)
    y_conv = jnp.einsum('btc,cd->btd', o_conv * jax.nn.silu(g_conv), w_conv_out)
    y_mem = jnp.einsum('btc,cd->btd', o_mem * jax.nn.silu(g_mem), w_mem_out)
    gates = jax.nn.sigmoid(jnp.einsum('btd,dn->btn', x, w_merge) + b_merge)
    g1, g2, g3 = jnp.split(gates, N_BRANCH, axis=-1)
    m = g1 * y_fox + g2 * y_conv + g3 * y_mem
    h = jnp.einsum('btd,de->bte', m, w_o)
    return _layernorm(DN_ALPHA * x + h, ln_g, ln_b)


def setup_inputs(seed: int = 0) -> dict:
    key = jax.random.key(seed)
    ks = jax.random.split(key, 24)
    f32 = jnp.float32
    L = DEPTH

    def nrm(k, shape, scale):
        return jax.random.normal(k, shape, f32) * scale

    return {
        "x_prompt": nrm(ks[0], (BATCH, SEQ, D_MODEL), 1.0),
        "x_sample": nrm(ks[1], (DEC_BATCH, DEC_SEQ, D_MODEL), 1.0),
        "mem_prompt": nrm(ks[2], (BATCH, MEM_TOKENS, D_MODEL), 1.0),
        "cache_fox_k": nrm(ks[3], (L, DEC_BATCH, PAST_LEN, FOX_HEADS, FOX_HEAD_DIM), 1.0),
        "cache_fox_v": nrm(ks[4], (L, DEC_BATCH, PAST_LEN, FOX_HEADS, FOX_HEAD_DIM), 1.0),
        "cache_fox_logf": jax.nn.log_sigmoid(nrm(ks[5], (L, DEC_BATCH, PAST_LEN, FOX_HEADS), 1.0) + 3.0),
        "state_conv": nrm(ks[6], (L, DEC_BATCH, CONV_K - 1, CONV_WIDTH), 1.0),
        "cache_mem_k": nrm(ks[7], (L, DEC_BATCH, MEM_TOKENS, MEM_HEADS, MEM_HEAD_DIM), 1.0),
        "cache_mem_v": nrm(ks[8], (L, DEC_BATCH, MEM_TOKENS, MEM_HEADS, MEM_HEAD_DIM), 1.0),
        "w_in": nrm(ks[9], (L, D_MODEL, IN_WIDTH), D_MODEL ** -0.5),
        "fox_bf": jnp.linspace(1.0, 4.0, FOX_HEADS)[None, :] + nrm(ks[10], (L, FOX_HEADS), 0.1),
        "conv_w": nrm(ks[11], (L, CONV_K, CONV_WIDTH), CONV_K ** -0.5),
        "conv_b": nrm(ks[12], (L, CONV_WIDTH), 0.01),
        "w_mem_kv": nrm(ks[13], (L, D_MODEL, 2 * MEM_WIDTH), D_MODEL ** -0.5),
        "w_fox_out": nrm(ks[14], (L, FOX_WIDTH, D_MODEL), FOX_WIDTH ** -0.5 * DN_BETA),
        "w_conv_out": nrm(ks[15], (L, CONV_WIDTH, D_MODEL), CONV_WIDTH ** -0.5 * DN_BETA),
        "w_mem_out": nrm(ks[16], (L, MEM_WIDTH, D_MODEL), MEM_WIDTH ** -0.5 * DN_BETA),
        "w_merge": nrm(ks[17], (L, D_MODEL, N_BRANCH * D_MODEL), D_MODEL ** -0.5),
        "b_merge": nrm(ks[18], (L, N_BRANCH * D_MODEL), 0.01),
        "w_o": nrm(ks[19], (L, D_MODEL, D_MODEL), D_MODEL ** -0.5 * DN_BETA),
        "ln_g": 1.0 + nrm(ks[20], (L, D_MODEL), 0.01),
        "ln_b": nrm(ks[21], (L, D_MODEL), 0.01),
    }


def reference(x_prompt, x_sample, mem_prompt, cache_fox_k, cache_fox_v, cache_fox_logf, state_conv,
              cache_mem_k, cache_mem_v, w_in, fox_bf, conv_w, conv_b, w_mem_kv, w_fox_out, w_conv_out,
              w_mem_out, w_merge, b_merge, w_o, ln_g, ln_b):
    hp, hs = x_prompt, x_sample
    fk_p, fv_p, fl_p, cv_p, mk_p, mv_p = [], [], [], [], [], []
    fk_s, fv_s, fl_s, cv_s = [], [], [], []
    for l in range(DEPTH):
        q, k, v, logf, g_fox, b_gate, u, g_conv, mq, g_mem = _project(hp, w_in[l], fox_bf[l])
        o_fox = _fox_prompt(q, k, v, logf)
        prev0 = jnp.zeros((hp.shape[0], CONV_K - 1, CONV_WIDTH), hp.dtype)
        c, tail = _short_conv(u, prev0, conv_w[l], conv_b[l])
        mk, mv = _mem_kv(mem_prompt, w_mem_kv[l])
        o_mem = _mem_attn(mq, mk, mv)
        hp_next = _combine(hp, o_fox, g_fox, b_gate * c, g_conv, o_mem, g_mem,
                           w_fox_out[l], w_conv_out[l], w_mem_out[l], w_merge[l], b_merge[l], w_o[l],
                           ln_g[l], ln_b[l])
        fk_p.append(k); fv_p.append(v); fl_p.append(logf); cv_p.append(tail)
        mk_p.append(mk); mv_p.append(mv)
        hp = hp_next
        q, k, v, logf, g_fox, b_gate, u, g_conv, mq, g_mem = _project(hs, w_in[l], fox_bf[l])
        o_fox = _fox_sample(q, k, v, logf, cache_fox_k[l], cache_fox_v[l], cache_fox_logf[l])
        c, tail = _short_conv(u, state_conv[l], conv_w[l], conv_b[l])
        o_mem = _mem_attn(mq, cache_mem_k[l], cache_mem_v[l])
        hs_next = _combine(hs, o_fox, g_fox, b_gate * c, g_conv, o_mem, g_mem,
                           w_fox_out[l], w_conv_out[l], w_mem_out[l], w_merge[l], b_merge[l], w_o[l],
                           ln_g[l], ln_b[l])
        fk_s.append(k); fv_s.append(v); fl_s.append(logf); cv_s.append(tail)
        hs = hs_next
    return (hp, hs,
            jnp.stack(fk_p), jnp.stack(fv_p), jnp.stack(fl_p), jnp.stack(cv_p),
            jnp.stack(mk_p), jnp.stack(mv_p),
            jnp.stack(fk_s), jnp.stack(fv_s), jnp.stack(fl_s), jnp.stack(cv_s))
```

```python
import functools

import jax
import jax.numpy as jnp
from jax import lax
from jax.experimental import pallas as pl
from jax.experimental.pallas import tpu as pltpu

D_MODEL = 2048
FOX_HEADS = 12
FOX_HEAD_DIM = 128
FOX_WIDTH = FOX_HEADS * FOX_HEAD_DIM
CONV_WIDTH = 1536
CONV_K = 3
MEM_TOKENS = 256
MEM_HEADS = 4
MEM_HEAD_DIM = 256
MEM_WIDTH = MEM_HEADS * MEM_HEAD_DIM
N_BRANCH = 3
LN_EPS = 1e-5
NEG_INF = -1e30

LANES = 128
VMEM_LIMIT = 56 * 1024 * 1024
CUMSUM_BLOCK = 256
F32 = jnp.float32
BF16 = jnp.bfloat16

_NT = (((1,), (1,)), ((), ()))


def _params(*sem):
    return pltpu.CompilerParams(dimension_semantics=sem, vmem_limit_bytes=VMEM_LIMIT)


def _silu(g):
    return g / (1.0 + jnp.exp(-g))


def _sigmoid(g):
    return 1.0 / (1.0 + jnp.exp(-g))


def _proj_kernel(x_ref, w_ref, scale_ref, o_ref):
    acc = jnp.dot(x_ref[...], w_ref[...], preferred_element_type=F32)
    o_ref[...] = (acc * scale_ref[...]).astype(o_ref.dtype)


def _project(x16, w16, scale, out_dtype, tm, tn, name):
    m, k = x16.shape
    n = w16.shape[1]
    return pl.pallas_call(
        _proj_kernel,
        grid=(m // tm, n // tn),
        in_specs=[pl.BlockSpec((tm, k), lambda i, j: (i, 0)),
                  pl.BlockSpec((k, tn), lambda i, j: (0, j)),
                  pl.BlockSpec((1, tn), lambda i, j: (0, j))],
        out_specs=pl.BlockSpec((tm, tn), lambda i, j: (i, j)),
        out_shape=jax.ShapeDtypeStruct((m, n), out_dtype),
        compiler_params=_params("parallel", "parallel"),
        name=name,
    )(x16, w16, scale)


def _cumsum_rows(x):
    s = x.shape[0]
    r = lax.broadcasted_iota(jnp.int32, (CUMSUM_BLOCK, CUMSUM_BLOCK), 0)
    c = lax.broadcasted_iota(jnp.int32, (CUMSUM_BLOCK, CUMSUM_BLOCK), 1)
    tri = (r >= c).astype(BF16)
    carry = jnp.zeros((1, x.shape[1]), F32)
    out = []
    for start in range(0, s, CUMSUM_BLOCK):
        n = min(CUMSUM_BLOCK, s - start)
        xb = x[start:start + n]
        hi = xb.astype(BF16)
        rem = xb - hi.astype(F32)
        mid = rem.astype(BF16)
        lo = (rem - mid.astype(F32)).astype(BF16)
        t = tri[:n, :n]
        cb = (jnp.dot(t, hi, preferred_element_type=F32)
              + jnp.dot(t, mid, preferred_element_type=F32)
              + jnp.dot(t, lo, preferred_element_type=F32)) + carry
        carry = cb[n - 1:n]
        out.append(cb)
    return jnp.concatenate(out, axis=0) if len(out) > 1 else out[0]


def _split3(col):
    hi = col.astype(BF16).astype(F32)
    rem = col - hi
    mid = rem.astype(BF16).astype(F32)
    lo = (rem - mid).astype(BF16).astype(F32)
    return hi, mid, lo


def _fox_prep_kernel(*refs, past_len):
    if past_len:
        x_ref, wf_ref, bf_ref, past_ref, logf_ref, qa_ref, ka_ref = refs
    else:
        x_ref, wf_ref, bf_ref, logf_ref, qa_ref, ka_ref = refs
    z = jnp.dot(x_ref[...], wf_ref[...], preferred_element_type=F32) + bf_ref[...]
    logf = jnp.minimum(z, 0.0) - jnp.log1p(jnp.exp(-jnp.abs(z)))
    logf_ref[...] = logf
    full = jnp.concatenate([past_ref[0], logf], axis=0) if past_len else logf
    cum = _cumsum_rows(full)
    lane = lax.broadcasted_iota(jnp.int32, (1, LANES), 1)
    for h in range(FOX_HEADS):
        hi, mid, lo = _split3(cum[:, h:h + 1])
        ka = jnp.where(lane < 3, 1.0,
                       jnp.where(lane == 3, -hi,
                                 jnp.where(lane == 4, -mid,
                                           jnp.where(lane == 5, -lo, 0.0))))
        ka_ref[:, h * LANES:(h + 1) * LANES] = ka.astype(BF16)
        hq, mq, lq = hi[past_len:], mid[past_len:], lo[past_len:]
        qa = jnp.where(lane == 0, hq,
                       jnp.where(lane == 1, mq,
                                 jnp.where(lane == 2, lq,
                                           jnp.where(lane < 6, 1.0, 0.0))))
        qa_ref[:, h * LANES:(h + 1) * LANES] = qa.astype(BF16)


def _fox_prep(x16, wf16, bf_pad, past_logf, batch):
    m = x16.shape[0]
    t = m // batch
    past_len = 0 if past_logf is None else past_logf.shape[1]
    sk = past_len + t
    in_specs = [pl.BlockSpec((t, D_MODEL), lambda b: (b, 0)),
                pl.BlockSpec((D_MODEL, LANES), lambda b: (0, 0)),
                pl.BlockSpec((1, LANES), lambda b: (0, 0))]
    args = [x16, wf16, bf_pad]
    if past_len:
        in_specs.append(pl.BlockSpec((1, past_len, LANES), lambda b: (b, 0, 0)))
        args.append(past_logf)
    return pl.pallas_call(
        functools.partial(_fox_prep_kernel, past_len=past_len),
        grid=(batch,),
        in_specs=in_specs,
        out_specs=[pl.BlockSpec((t, LANES), lambda b: (b, 0)),
                   pl.BlockSpec((t, FOX_WIDTH), lambda b: (b, 0)),
                   pl.BlockSpec((sk, FOX_WIDTH), lambda b: (b, 0))],
        out_shape=[jax.ShapeDtypeStruct((m, LANES), F32),
                   jax.ShapeDtypeStruct((m, FOX_WIDTH), BF16),
                   jax.ShapeDtypeStruct((batch * sk, FOX_WIDTH), BF16)],
        compiler_params=_params("parallel"),
        name="fox_prep",
    )(*args)


def _fox_attn_kernel(*refs, past_len, tq):
    if past_len:
        q_ref, qa_ref, k_ref, v_ref, ka_ref, g_ref, kp_ref, vp_ref, o_ref, qq, kk, vv = refs
        kk[:past_len, :LANES] = kp_ref[...].astype(BF16)
        vv[:past_len, :] = vp_ref[...].astype(BF16)
    else:
        q_ref, qa_ref, k_ref, v_ref, ka_ref, g_ref, o_ref, qq, kk, vv = refs
    t = q_ref.shape[0]
    qq[:, :LANES] = q_ref[...]
    qq[:, LANES:] = qa_ref[...]
    kk[past_len:, :LANES] = k_ref[...].astype(BF16)
    kk[:, LANES:] = ka_ref[...]
    vv[past_len:, :] = v_ref[...].astype(BF16)

    row = lax.broadcasted_iota(jnp.int32, (tq, tq), 0)
    col = lax.broadcasted_iota(jnp.int32, (tq, tq), 1)
    causal = col <= row
    for i in range(t // tq):
        r0 = i * tq
        kv_len = past_len + r0
        qi = qq[r0:r0 + tq, :]
        s_d = lax.dot_general(qi, kk[kv_len:kv_len + tq, :], _NT, preferred_element_type=F32)
        s_d = jnp.where(causal, s_d, NEG_INF)
        m = jnp.max(s_d, axis=-1, keepdims=True)
        if kv_len:
            s_p = lax.dot_general(qi, kk[:kv_len, :], _NT, preferred_element_type=F32)
            m = jnp.maximum(m, jnp.max(s_p, axis=-1, keepdims=True))
            p_p = jnp.exp(s_p - m)
        p_d = jnp.exp(s_d - m)
        l = jnp.sum(p_d, axis=-1, keepdims=True)
        o = jnp.dot(p_d.astype(BF16), vv[kv_len:kv_len + tq, :], preferred_element_type=F32)
        if kv_len:
            l = l + jnp.sum(p_p, axis=-1, keepdims=True)
            o = o + jnp.dot(p_p.astype(BF16), vv[:kv_len, :], preferred_element_type=F32)
        g = g_ref[r0:r0 + tq, :]
        o_ref[r0:r0 + tq, :] = (o / l * _silu(g)).astype(o_ref.dtype)


def _fox_attn(qm16, qa16, k32, v32, ka16, gates, k_past, v_past, batch, tq):
    m = qm16.shape[0]
    t = m // batch
    past_len = 0 if k_past is None else k_past.shape[0] // batch
    sk = past_len + t
    off = MEM_WIDTH // LANES
    head = lambda b, h: (b, h)
    head_off = lambda b, h: (b, off + h)
    in_specs = [pl.BlockSpec((t, LANES), head_off),
                pl.BlockSpec((t, LANES), head),
                pl.BlockSpec((t, LANES), head),
                pl.BlockSpec((t, LANES), head),
                pl.BlockSpec((sk, LANES), head),
                pl.BlockSpec((t, LANES), head_off)]
    args = [qm16, qa16, k32, v32, ka16, gates]
    if past_len:
        in_specs += [pl.BlockSpec((past_len, LANES), head), pl.BlockSpec((past_len, LANES), head)]
        args += [k_past, v_past]
    return pl.pallas_call(
        functools.partial(_fox_attn_kernel, past_len=past_len, tq=tq),
        grid=(batch, FOX_HEADS),
        in_specs=in_specs,
        out_specs=pl.BlockSpec((t, LANES), head),
        out_shape=jax.ShapeDtypeStruct((m, FOX_WIDTH), BF16),
        scratch_shapes=[pltpu.VMEM((t, 2 * LANES), BF16),
                        pltpu.VMEM((sk, 2 * LANES), BF16),
                        pltpu.VMEM((sk, LANES), BF16)],
        compiler_params=_params("parallel", "parallel"),
        name="fox_attn",
    )(*args)


def _conv_kernel(x_ref, w_ref, prev_ref, cw_ref, cb_ref, a_ref, tail_ref, *, nb, tc):
    z = jnp.dot(x_ref[...], w_ref[0], preferred_element_type=F32)
    t = x_ref.shape[0] // nb
    w0, w1, w2 = cw_ref[0:1, :], cw_ref[1:2, :], cw_ref[2:3, :]
    bias = cb_ref[...]
    rowi = lax.broadcasted_iota(jnp.int32, (t, tc), 0)
    for b in range(nb):
        zb = z[b * t:(b + 1) * t]
        b_gate, u, g = zb[:, :tc], zb[:, tc:2 * tc] * zb[:, 2 * tc:3 * tc], zb[:, 3 * tc:]
        p0, p1 = prev_ref[b, 0:1, :], prev_ref[b, 1:2, :]
        u1 = jnp.where(rowi == 0, p1, pltpu.roll(u, 1, 0))
        u2 = jnp.where(rowi == 0, p0, jnp.where(rowi == 1, p1, pltpu.roll(u, 2, 0)))
        c = w0 * u2 + w1 * u1 + w2 * u + bias
        a_ref[b * t:(b + 1) * t, :] = (b_gate * c * _silu(g)).astype(a_ref.dtype)
        tail_ref[b] = u[t - (CONV_K - 1):, :]


def _conv_branch(x16, w4, prev, conv_w, conv_b, batch, nb, tc):
    m = x16.shape[0]
    t = m // batch
    return pl.pallas_call(
        functools.partial(_conv_kernel, nb=nb, tc=tc),
        grid=(CONV_WIDTH // tc, batch // nb),
        in_specs=[pl.BlockSpec((nb * t, D_MODEL), lambda c, b: (b, 0)),
                  pl.BlockSpec((1, D_MODEL, 4 * tc), lambda c, b: (c, 0, 0)),
                  pl.BlockSpec((nb, CONV_K - 1, tc), lambda c, b: (b, 0, c)),
                  pl.BlockSpec((CONV_K, tc), lambda c, b: (0, c)),
                  pl.BlockSpec((1, tc), lambda c, b: (0, c))],
        out_specs=[pl.BlockSpec((nb * t, tc), lambda c, b: (b, c)),
                   pl.BlockSpec((nb, CONV_K - 1, tc), lambda c, b: (b, 0, c))],
        out_shape=[jax.ShapeDtypeStruct((m, CONV_WIDTH), BF16),
                   jax.ShapeDtypeStruct((batch, CONV_K - 1, CONV_WIDTH), F32)],
        compiler_params=_params("parallel", "parallel"),
        name="conv_branch",
    )(x16, w4, prev, conv_w, conv_b)


def _mem_kv_kernel(x_ref, w_ref, o32_ref, o16_ref):
    acc = jnp.dot(x_ref[...], w_ref[...], preferred_element_type=F32)
    o32_ref[0] = acc
    o16_ref[...] = acc.astype(BF16)


def _mem_kv(mem16, w16, tm):
    m, k = mem16.shape
    return pl.pallas_call(
        _mem_kv_kernel,
        grid=(m // tm, 2),
        in_specs=[pl.BlockSpec((tm, k), lambda i, j: (i, 0)),
                  pl.BlockSpec((k, MEM_WIDTH), lambda i, j: (0, j))],
        out_specs=[pl.BlockSpec((1, tm, MEM_WIDTH), lambda i, j: (j, i, 0)),
                   pl.BlockSpec((tm, MEM_WIDTH), lambda i, j: (i, j))],
        out_shape=[jax.ShapeDtypeStruct((2, m, MEM_WIDTH), F32),
                   jax.ShapeDtypeStruct((m, 2 * MEM_WIDTH), BF16)],
        compiler_params=_params("parallel", "parallel"),
        name="mem_kv",
    )(mem16, w16)


def _mem_attn_kernel(q_ref, k_ref, v_ref, g_ref, o_ref):
    for h in range(MEM_HEADS):
        cs = slice(h * MEM_HEAD_DIM, (h + 1) * MEM_HEAD_DIM)
        s = lax.dot_general(q_ref[:, cs], k_ref[:, cs], _NT, preferred_element_type=F32)
        m = jnp.max(s, axis=-1, keepdims=True)
        p = jnp.exp(s - m)
        l = jnp.sum(p, axis=-1, keepdims=True)
        o = jnp.dot(p.astype(BF16), v_ref[:, cs], preferred_element_type=F32)
        o_ref[:, cs] = (o / l * _silu(g_ref[:, cs])).astype(o_ref.dtype)


def _mem_attn(qm16, mkv16, gates, batch, tm):
    m = qm16.shape[0]
    t = m // batch
    nt = t // tm
    return pl.pallas_call(
        _mem_attn_kernel,
        grid=(batch, nt),
        in_specs=[pl.BlockSpec((tm, MEM_WIDTH), lambda b, i: (b * nt + i, 0)),
                  pl.BlockSpec((MEM_TOKENS, MEM_WIDTH), lambda b, i: (b, 0)),
                  pl.BlockSpec((MEM_TOKENS, MEM_WIDTH), lambda b, i: (b, 1)),
                  pl.BlockSpec((tm, MEM_WIDTH), lambda b, i: (b * nt + i, 0))],
        out_specs=pl.BlockSpec((tm, MEM_WIDTH), lambda b, i: (b * nt + i, 0)),
        out_shape=jax.ShapeDtypeStruct((m, MEM_WIDTH), BF16),
        compiler_params=_params("parallel", "parallel"),
        name="mem_attn",
    )(qm16, mkv16, mkv16, gates)


def _merge_kernel(af_ref, ac_ref, am_ref, x_ref, wf_ref, wc_ref, wm_ref,
                  g1w_ref, g2w_ref, g3w_ref, g1b_ref, g2b_ref, g3b_ref, o_ref):
    x = x_ref[...]
    g1 = _sigmoid(jnp.dot(x, g1w_ref[...], preferred_element_type=F32) + g1b_ref[...])
    acc = g1 * jnp.dot(af_ref[...], wf_ref[...], preferred_element_type=F32)
    g2 = _sigmoid(jnp.dot(x, g2w_ref[...], preferred_element_type=F32) + g2b_ref[...])
    acc = acc + g2 * jnp.dot(ac_ref[...], wc_ref[...], preferred_element_type=F32)
    g3 = _sigmoid(jnp.dot(x, g3w_ref[...], preferred_element_type=F32) + g3b_ref[...])
    acc = acc + g3 * jnp.dot(am_ref[...], wm_ref[...], preferred_element_type=F32)
    o_ref[...] = acc.astype(o_ref.dtype)


def _merge(a_fox, a_conv, a_mem, x16, wfo, wco, wmo, wmerge, bmerge, tm, tn):
    m = x16.shape[0]
    nj = D_MODEL // tn
    row = lambda j, i: (i, 0)
    colb = lambda k: (lambda j, i: (0, j + k * nj))
    return pl.pallas_call(
        _merge_kernel,
        grid=(nj, m // tm),
        in_specs=[pl.BlockSpec((tm, FOX_WIDTH), row),
                  pl.BlockSpec((tm, CONV_WIDTH), row),
                  pl.BlockSpec((tm, MEM_WIDTH), row),
                  pl.BlockSpec((tm, D_MODEL), row),
                  pl.BlockSpec((FOX_WIDTH, tn), colb(0)),
                  pl.BlockSpec((CONV_WIDTH, tn), colb(0)),
                  pl.BlockSpec((MEM_WIDTH, tn), colb(0)),
                  pl.BlockSpec((D_MODEL, tn), colb(0)),
                  pl.BlockSpec((D_MODEL, tn), colb(1)),
                  pl.BlockSpec((D_MODEL, tn), colb(2)),
                  pl.BlockSpec((1, tn), colb(0)),
                  pl.BlockSpec((1, tn), colb(1)),
                  pl.BlockSpec((1, tn), colb(2))],
        out_specs=pl.BlockSpec((tm, tn), lambda j, i: (i, j)),
        out_shape=jax.ShapeDtypeStruct((m, D_MODEL), BF16),
        compiler_params=_params("parallel", "parallel"),
        name="merge",
    )(a_fox, a_conv, a_mem, x16, wfo, wco, wmo, wmerge, wmerge, wmerge, bmerge, bmerge, bmerge)


def _out_kernel(m_ref, w_ref, x_ref, g_ref, b_ref, o_ref, *, alpha):
    h = jnp.dot(m_ref[...], w_ref[...], preferred_element_type=F32)
    r = alpha * x_ref[...] + h
    mu = jnp.mean(r, axis=-1, keepdims=True)
    d = r - mu
    var = jnp.mean(d * d, axis=-1, keepdims=True)
    o_ref[...] = d * lax.rsqrt(var + LN_EPS) * g_ref[...] + b_ref[...]


def _out_proj_norm(m16, wo16, x32, ln_g, ln_b, alpha, tm):
    m = x32.shape[0]
    return pl.pallas_call(
        functools.partial(_out_kernel, alpha=alpha),
        grid=(m // tm,),
        in_specs=[pl.BlockSpec((tm, D_MODEL), lambda i: (i, 0)),
                  pl.BlockSpec((D_MODEL, D_MODEL), lambda i: (0, 0)),
                  pl.BlockSpec((tm, D_MODEL), lambda i: (i, 0)),
                  pl.BlockSpec((1, D_MODEL), lambda i: (0, 0)),
                  pl.BlockSpec((1, D_MODEL), lambda i: (0, 0))],
        out_specs=pl.BlockSpec((tm, D_MODEL), lambda i: (i, 0)),
        out_shape=jax.ShapeDtypeStruct((m, D_MODEL), F32),
        compiler_params=_params("parallel"),
        name="out_proj_norm",
    )(m16, wo16, x32, ln_g, ln_b)


def _layer(x32, batch, wts, alpha, mkv16, prev_conv, k_past, v_past, past_logf, tiles):
    x16 = x32.astype(BF16)
    tm = tiles["tm"]
    qm16 = _project(x16, wts["w_qm"], wts["s_qm"], BF16, tm, 1280, "proj_q")
    k32 = _project(x16, wts["w_k"], wts["ones_k"], F32, tm, FOX_WIDTH, "proj_k")
    v32 = _project(x16, wts["w_v"], wts["ones_k"], F32, tm, FOX_WIDTH, "proj_v")
    gates = _project(x16, wts["w_g"], wts["ones_g"], F32, tm, 1280, "proj_g")
    logf, qa16, ka16 = _fox_prep(x16, wts["w_f"], wts["b_f"], past_logf, batch)
    a_fox = _fox_attn(qm16, qa16, k32, v32, ka16, gates, k_past, v_past, batch, tiles["tq"])
    a_conv, tail = _conv_branch(x16, wts["w_conv"], prev_conv, wts["conv_w"], wts["conv_b"],
                                batch, tiles["conv_nb"], tiles["conv_tc"])
    a_mem = _mem_attn(qm16, mkv16, gates, batch, tiles["mem_tm"])
    m16 = _merge(a_fox, a_conv, a_mem, x16, wts["w_fox_out"], wts["w_conv_out"], wts["w_mem_out"],
                 wts["w_merge"], wts["b_merge"], tiles["merge_tm"], 512)
    y = _out_proj_norm(m16, wts["w_o"], x32, wts["ln_g"], wts["ln_b"], alpha, tiles["out_tm"])
    return y, k32, v32, logf, tail


def _layer_weights(l, w_in, fox_bf, conv_w, conv_b, w_mem_kv, w_fox_out, w_conv_out, w_mem_out,
                   w_merge, b_merge, w_o, ln_g, ln_b, conv_tc):
    w = w_in[l]
    sizes = (FOX_WIDTH, FOX_WIDTH, FOX_WIDTH, FOX_HEADS, FOX_WIDTH,
             CONV_WIDTH, CONV_WIDTH, CONV_WIDTH, CONV_WIDTH, MEM_WIDTH, MEM_WIDTH)
    offs = [0]
    for s in sizes:
        offs.append(offs[-1] + s)
    col = lambda i: w[:, offs[i]:offs[i + 1]]
    wq, wk, wv, wf, wgf, wcb, wcc, wch, wcg, wmq, wgm = [col(i) for i in range(len(sizes))]
    nct = CONV_WIDTH // conv_tc
    w_conv = jnp.stack([c.reshape(D_MODEL, nct, conv_tc) for c in (wcb, wcc, wch, wcg)], axis=2)
    w_conv = jnp.transpose(w_conv.reshape(D_MODEL, nct, 4 * conv_tc), (1, 0, 2))
    s_qm = jnp.concatenate([jnp.full((1, MEM_WIDTH), MEM_HEAD_DIM ** -0.5, F32),
                            jnp.full((1, FOX_WIDTH), FOX_HEAD_DIM ** -0.5, F32)], axis=1)
    return {
        "w_qm": jnp.concatenate([wmq, wq], axis=1).astype(BF16),
        "s_qm": s_qm,
        "w_k": wk.astype(BF16),
        "w_v": wv.astype(BF16),
        "ones_k": jnp.ones((1, FOX_WIDTH), F32),
        "w_g": jnp.concatenate([wgm, wgf], axis=1).astype(BF16),
        "ones_g": jnp.ones((1, MEM_WIDTH + FOX_WIDTH), F32),
        "w_f": jnp.pad(wf, ((0, 0), (0, LANES - FOX_HEADS))).astype(BF16),
        "b_f": jnp.pad(fox_bf[l].astype(F32), (0, LANES - FOX_HEADS)).reshape(1, LANES),
        "w_conv": w_conv.astype(BF16),
        "conv_w": conv_w[l],
        "conv_b": conv_b[l].reshape(1, CONV_WIDTH),
        "w_mem_kv": w_mem_kv[l].astype(BF16),
        "w_fox_out": w_fox_out[l].astype(BF16),
        "w_conv_out": w_conv_out[l].astype(BF16),
        "w_mem_out": w_mem_out[l].astype(BF16),
        "w_merge": w_merge[l].astype(BF16),
        "b_merge": b_merge[l].reshape(1, N_BRANCH * D_MODEL),
        "w_o": w_o[l].astype(BF16),
        "ln_g": ln_g[l].reshape(1, D_MODEL),
        "ln_b": ln_b[l].reshape(1, D_MODEL),
    }


def kernel(x_prompt, x_sample, mem_prompt, cache_fox_k, cache_fox_v, cache_fox_logf, state_conv,
           cache_mem_k, cache_mem_v, w_in, fox_bf, conv_w, conv_b, w_mem_kv, w_fox_out, w_conv_out,
           w_mem_out, w_merge, b_merge, w_o, ln_g, ln_b):
    depth = w_in.shape[0]
    alpha = (2 * depth) ** 0.25
    bp, sp, _ = x_prompt.shape
    bs, ts, _ = x_sample.shape
    past = cache_fox_k.shape[2]
    conv_tc = 256
    prompt_tiles = dict(tm=1024, tq=256, conv_nb=1, conv_tc=conv_tc, mem_tm=1024, merge_tm=512, out_tm=512)
    sample_tiles = dict(tm=bs * ts, tq=ts, conv_nb=bs, conv_tc=conv_tc, mem_tm=ts, merge_tm=bs * ts,
                        out_tm=bs * ts)

    hp = x_prompt.reshape(bp * sp, D_MODEL)
    hs = x_sample.reshape(bs * ts, D_MODEL)
    mem16 = mem_prompt.reshape(bp * MEM_TOKENS, D_MODEL).astype(BF16)
    outs = [[] for _ in range(10)]
    for l in range(depth):
        wts = _layer_weights(l, w_in, fox_bf, conv_w, conv_b, w_mem_kv, w_fox_out, w_conv_out,
                             w_mem_out, w_merge, b_merge, w_o, ln_g, ln_b, conv_tc)
        mkv32, mkv16 = _mem_kv(mem16, wts["w_mem_kv"], 1024)
        prev0 = jnp.zeros((bp, CONV_K - 1, CONV_WIDTH), F32)
        hp, k_p, v_p, lf_p, tail_p = _layer(hp, bp, wts, alpha, mkv16, prev0, None, None, None,
                                            prompt_tiles)
        ck = cache_fox_k[l].reshape(bs * past, FOX_WIDTH)
        cv = cache_fox_v[l].reshape(bs * past, FOX_WIDTH)
        clf = jnp.pad(cache_fox_logf[l].astype(F32), ((0, 0), (0, 0), (0, LANES - FOX_HEADS)))
        cmkv16 = jnp.concatenate([cache_mem_k[l].reshape(bs * MEM_TOKENS, MEM_WIDTH),
                                  cache_mem_v[l].reshape(bs * MEM_TOKENS, MEM_WIDTH)], axis=1).astype(BF16)
        hs, k_s, v_s, lf_s, tail_s = _layer(hs, bs, wts, alpha, cmkv16, state_conv[l].astype(F32),
                                            ck, cv, clf, sample_tiles)
        vals = (k_p.reshape(bp, sp, FOX_HEADS, FOX_HEAD_DIM), v_p.reshape(bp, sp, FOX_HEADS, FOX_HEAD_DIM),
                lf_p[:, :FOX_HEADS].reshape(bp, sp, FOX_HEADS), tail_p,
                mkv32[0].reshape(bp, MEM_TOKENS, MEM_HEADS, MEM_HEAD_DIM),
                mkv32[1].reshape(bp, MEM_TOKENS, MEM_HEADS, MEM_HEAD_DIM),
                k_s.reshape(bs, ts, FOX_HEADS, FOX_HEAD_DIM), v_s.reshape(bs, ts, FOX_HEADS, FOX_HEAD_DIM),
                lf_s[:, :FOX_HEADS].reshape(bs, ts, FOX_HEADS), tail_s)
        for acc, val in zip(outs, vals):
            acc.append(val)
    return (hp.reshape(bp, sp, D_MODEL), hs.reshape(bs, ts, D_MODEL)) + tuple(jnp.stack(o) for o in outs)
```

```python
import functools
import math

import numpy as np
import jax
import jax.numpy as jnp
from jax import lax
from jax.experimental import pallas as pl
from jax.experimental.pallas import tpu as pltpu

D_MODEL = 2048
FOX_HEADS = 12
FOX_HEAD_DIM = 128
FOX_WIDTH = FOX_HEADS * FOX_HEAD_DIM
CONV_WIDTH = 1536
CONV_K = 3
MEM_TOKENS = 256
MEM_HEADS = 4
MEM_HEAD_DIM = 256
MEM_WIDTH = MEM_HEADS * MEM_HEAD_DIM
N_BRANCH = 3
LN_EPS = 1e-5
NEG_INF = -1e30
LOG2E = math.log2(math.e)

LANES = 128
VMEM_LIMIT = 56 * 1024 * 1024
CUMSUM_BLOCK = 256
WCAST_COLS = 512
F32 = jnp.float32
BF16 = jnp.bfloat16

OFF_F = 3 * FOX_WIDTH
OFF_B = OFF_F + FOX_HEADS
WB_GFOX, WB_CB, WB_CC, WB_CH, WB_CG = (i * FOX_WIDTH for i in range(5))
WB_MQ = 5 * FOX_WIDTH
WB_GMEM = WB_MQ + MEM_WIDTH
WB_WIDTH = WB_GMEM + MEM_WIDTH

_NT = (((1,), (1,)), ((), ()))


def _params(*sem):
    return pltpu.CompilerParams(dimension_semantics=sem, vmem_limit_bytes=VMEM_LIMIT)


def _silu(g):
    return g / (1.0 + jnp.exp(-g))


def _sigmoid(g):
    return 1.0 / (1.0 + jnp.exp(-g))


def _wcast_kernel(*refs, shift):
    if shift:
        a_ref, b_ref, o_ref = refs
        x = jnp.concatenate([a_ref[...], b_ref[...]], axis=1)
        y = pltpu.roll(x, x.shape[1] - shift, 1)[:, :a_ref.shape[1]]
    else:
        a_ref, o_ref = refs
        y = a_ref[...]
    o_ref[...] = y.astype(BF16)


def _wcast(w, start, width):
    rows = w.shape[0]
    shift = start % LANES
    a0 = (start - shift) // WCAST_COLS
    assert (start - shift) % WCAST_COLS == 0 and width % WCAST_COLS == 0
    in_specs = [pl.BlockSpec((rows, WCAST_COLS), lambda j: (0, a0 + j))]
    args = [w]
    if shift:
        per = WCAST_COLS // LANES
        in_specs.append(pl.BlockSpec((rows, LANES), lambda j: (0, (a0 + j + 1) * per)))
        args.append(w)
    return pl.pallas_call(
        functools.partial(_wcast_kernel, shift=shift),
        grid=(width // WCAST_COLS,),
        in_specs=in_specs,
        out_specs=pl.BlockSpec((rows, WCAST_COLS), lambda j: (0, j)),
        out_shape=jax.ShapeDtypeStruct((rows, width), BF16),
        compiler_params=_params("parallel"),
        name="wcast",
    )(*args)


def _proj_kernel(x_ref, w_ref, o_ref, *, scale):
    acc = jnp.dot(x_ref[...], w_ref[...], preferred_element_type=F32)
    if scale != 1.0:
        acc = acc * scale
    o_ref[...] = acc.astype(o_ref.dtype)


def _project(x16, w16, col0, n, scale, out_dtype, tm, tn, name):
    m, k = x16.shape
    assert col0 % tn == 0 and n % tn == 0
    j0 = col0 // tn
    return pl.pallas_call(
        functools.partial(_proj_kernel, scale=scale),
        grid=(m // tm, n // tn),
        in_specs=[pl.BlockSpec((tm, k), lambda i, j: (i, 0)),
                  pl.BlockSpec((k, tn), lambda i, j: (0, j0 + j))],
        out_specs=pl.BlockSpec((tm, tn), lambda i, j: (i, j)),
        out_shape=jax.ShapeDtypeStruct((m, n), out_dtype),
        compiler_params=_params("parallel", "parallel"),
        name=name,
    )(x16, w16)


def _proj_heads_kernel(x_ref, w_ref, o_ref, *maybe_x16_ref, nb):
    x16 = x_ref[...].astype(BF16)
    if maybe_x16_ref:
        maybe_x16_ref[0][...] = x16
    acc = jnp.dot(x16, w_ref[...], preferred_element_type=F32)
    t = x_ref.shape[0] // nb
    for b in range(nb):
        for h in range(FOX_HEADS):
            o_ref[b, h] = acc[b * t:(b + 1) * t, h * LANES:(h + 1) * LANES]


def _project_heads(x, w16, col0, batch, tm, emit_x16, name):
    m, k = x.shape
    t = m // batch
    j0 = col0 // FOX_WIDTH
    if t >= tm:
        nb, tb, per = 1, tm, t // tm
        o_map = lambda i: (i // per, 0, i % per, 0)
    else:
        nb, tb = tm // t, t
        o_map = lambda i: (i, 0, 0, 0)
    out_specs = [pl.BlockSpec((nb, FOX_HEADS, tb, LANES), o_map)]
    out_shape = [jax.ShapeDtypeStruct((batch, FOX_HEADS, t, LANES), F32)]
    if emit_x16:
        out_specs.append(pl.BlockSpec((tm, k), lambda i: (i, 0)))
        out_shape.append(jax.ShapeDtypeStruct((m, k), BF16))
    res = pl.pallas_call(
        functools.partial(_proj_heads_kernel, nb=nb),
        grid=(m // tm,),
        in_specs=[pl.BlockSpec((tm, k), lambda i: (i, 0)),
                  pl.BlockSpec((k, FOX_WIDTH), lambda i: (0, j0))],
        out_specs=out_specs,
        out_shape=out_shape,
        compiler_params=_params("parallel"),
        name=name,
    )(x, w16)
    return res if emit_x16 else res[0]


def _split3(x):
    hi = x.astype(BF16)
    rem = x - hi.astype(F32)
    mid = rem.astype(BF16)
    lo = (rem - mid.astype(F32)).astype(BF16)
    return hi, mid, lo


def _cumsum_rows(x):
    s = x.shape[0]
    r = lax.broadcasted_iota(jnp.int32, (CUMSUM_BLOCK, CUMSUM_BLOCK), 0)
    c = lax.broadcasted_iota(jnp.int32, (CUMSUM_BLOCK, CUMSUM_BLOCK), 1)
    tri = (r >= c).astype(BF16)
    carry = jnp.zeros((1, x.shape[1]), F32)
    out = []
    for start in range(0, s, CUMSUM_BLOCK):
        n = min(CUMSUM_BLOCK, s - start)
        hi, mid, lo = _split3(x[start:start + n])
        t = tri[:n, :n]
        cb = (jnp.dot(t, hi, preferred_element_type=F32)
              + jnp.dot(t, mid, preferred_element_type=F32)
              + jnp.dot(t, lo, preferred_element_type=F32)) + carry
        carry = cb[n - 1:n]
        out.append(cb)
    return jnp.concatenate(out, axis=0) if len(out) > 1 else out[0]


def _fox_prep_kernel(*refs, past_len):
    if past_len:
        x_ref, wf_ref, bf_ref, past_ref, logf_ref, c3_ref = refs
    else:
        x_ref, wf_ref, bf_ref, logf_ref, c3_ref = refs
    z = jnp.dot(x_ref[...], wf_ref[...], preferred_element_type=F32) + bf_ref[...]
    logf = jnp.minimum(z, 0.0) - jnp.log1p(jnp.exp(-jnp.abs(z)))
    logf_ref[...] = logf
    full = jnp.concatenate([past_ref[0], logf], axis=0) if past_len else logf
    hi, mid, lo = _split3(_cumsum_rows(full) * LOG2E)
    c3_ref[:, 0 * LANES:1 * LANES] = hi
    c3_ref[:, 1 * LANES:2 * LANES] = mid
    c3_ref[:, 2 * LANES:3 * LANES] = lo


def _fox_prep(x16, wf16, bf_pad, past_logf, batch):
    m = x16.shape[0]
    t = m // batch
    past_len = 0 if past_logf is None else past_logf.shape[1]
    sk = past_len + t
    in_specs = [pl.BlockSpec((t, D_MODEL), lambda b: (b, 0)),
                pl.BlockSpec((D_MODEL, LANES), lambda b: (0, 0)),
                pl.BlockSpec((1, LANES), lambda b: (0, 0))]
    args = [x16, wf16, bf_pad]
    if past_len:
        in_specs.append(pl.BlockSpec((1, past_len, LANES), lambda b: (b, 0, 0)))
        args.append(past_logf)
    return pl.pallas_call(
        functools.partial(_fox_prep_kernel, past_len=past_len),
        grid=(batch,),
        in_specs=in_specs,
        out_specs=[pl.BlockSpec((t, LANES), lambda b: (b, 0)),
                   pl.BlockSpec((sk, 3 * LANES), lambda b: (b, 0))],
        out_shape=[jax.ShapeDtypeStruct((m, LANES), F32),
                   jax.ShapeDtypeStruct((batch * sk, 3 * LANES), BF16)],
        compiler_params=_params("parallel"),
        name="fox_prep",
    )(*args)


def _bias_selectors():
    sel = np.zeros((FOX_HEADS, 2, 3 * LANES, LANES), np.float32)
    for h in range(FOX_HEADS):
        for piece in range(3):
            sel[h, 0, piece * LANES + h, piece] = 1.0
            sel[h, 1, piece * LANES + h, 3 + piece] = -1.0
    ones = np.zeros((2, 1, LANES), np.float32)
    ones[0, 0, 3:6] = 1.0
    ones[1, 0, 0:3] = 1.0
    return jnp.asarray(sel, BF16), jnp.asarray(ones, F32)


def _fox_attn_kernel(*refs, past_len, tq):
    if past_len:
        q_ref, c3_ref, sel_ref, one_ref, k_ref, v_ref, g_ref, kp_ref, vp_ref, o_ref, qq, kk, vv = refs
        kk[:past_len, :LANES] = kp_ref[0, 0].astype(BF16)
        vv[:past_len, :] = vp_ref[0, 0].astype(BF16)
    else:
        q_ref, c3_ref, sel_ref, one_ref, k_ref, v_ref, g_ref, o_ref, qq, kk, vv = refs
    t = q_ref.shape[0]
    c3 = c3_ref[...]
    qq[:, :LANES] = q_ref[...]
    qq[:, LANES:] = (jnp.dot(c3[past_len:], sel_ref[0, 0], preferred_element_type=F32)
                     + one_ref[0]).astype(BF16)
    kk[past_len:, :LANES] = k_ref[0, 0].astype(BF16)
    kk[:, LANES:] = (jnp.dot(c3, sel_ref[0, 1], preferred_element_type=F32) + one_ref[1]).astype(BF16)
    vv[past_len:, :] = v_ref[0, 0].astype(BF16)

    row = lax.broadcasted_iota(jnp.int32, (tq, tq), 0)
    col = lax.broadcasted_iota(jnp.int32, (tq, tq), 1)
    causal = col <= row
    for i in range(t // tq):
        r0 = i * tq
        kv_len = past_len + r0
        qi = qq[r0:r0 + tq, :]
        s_d = lax.dot_general(qi, kk[kv_len:kv_len + tq, :], _NT, preferred_element_type=F32)
        s_d = jnp.where(causal, s_d, NEG_INF)
        m = jnp.max(s_d, axis=-1, keepdims=True)
        if kv_len:
            s_p = lax.dot_general(qi, kk[:kv_len, :], _NT, preferred_element_type=F32)
            m = jnp.maximum(m, jnp.max(s_p, axis=-1, keepdims=True))
            p_p = jnp.exp2(s_p - m)
        p_d = jnp.exp2(s_d - m)
        l = jnp.sum(p_d, axis=-1, keepdims=True)
        o = jnp.dot(p_d.astype(BF16), vv[kv_len:kv_len + tq, :], preferred_element_type=F32)
        if kv_len:
            l = l + jnp.sum(p_p, axis=-1, keepdims=True)
            o = o + jnp.dot(p_p.astype(BF16), vv[:kv_len, :], preferred_element_type=F32)
        g = g_ref[r0:r0 + tq, :]
        o_ref[r0:r0 + tq, :] = (o / l * _silu(g)).astype(o_ref.dtype)


def _fox_attn(q16, c3, sel, ones, k32, v32, g_fox, k_past, v_past, tq):
    batch, _, t, _ = k32.shape
    m = batch * t
    past_len = 0 if k_past is None else k_past.shape[2]
    sk = past_len + t
    tok = lambda b, h: (b, h)
    per_head = lambda b, h: (b, h, 0, 0)
    in_specs = [pl.BlockSpec((t, LANES), tok),
                pl.BlockSpec((sk, 3 * LANES), lambda b, h: (b, 0)),
                pl.BlockSpec((1, 2, 3 * LANES, LANES), lambda b, h: (h, 0, 0, 0)),
                pl.BlockSpec((2, 1, LANES), lambda b, h: (0, 0, 0)),
                pl.BlockSpec((1, 1, t, LANES), per_head),
                pl.BlockSpec((1, 1, t, LANES), per_head),
                pl.BlockSpec((t, LANES), tok)]
    args = [q16, c3, sel, ones, k32, v32, g_fox]
    if past_len:
        in_specs += [pl.BlockSpec((1, 1, past_len, LANES), per_head)] * 2
        args += [k_past, v_past]
    return pl.pallas_call(
        functools.partial(_fox_attn_kernel, past_len=past_len, tq=tq),
        grid=(batch, FOX_HEADS),
        in_specs=in_specs,
        out_specs=pl.BlockSpec((t, LANES), tok),
        out_shape=jax.ShapeDtypeStruct((m, FOX_WIDTH), BF16),
        scratch_shapes=[pltpu.VMEM((t, 2 * LANES), BF16),
                        pltpu.VMEM((sk, 2 * LANES), BF16),
                        pltpu.VMEM((sk, LANES), BF16)],
        compiler_params=_params("parallel", "parallel"),
        name="fox_attn",
    )(*args)


def _conv_kernel(x_ref, wb_ref, wc_ref, wh_ref, wg_ref, prev_ref, cw_ref, cb_ref, a_ref, tail_ref, *, nb):
    x = x_ref[...]
    zb = jnp.dot(x, wb_ref[...], preferred_element_type=F32)
    zu = (jnp.dot(x, wc_ref[...], preferred_element_type=F32)
          * jnp.dot(x, wh_ref[...], preferred_element_type=F32))
    zg = jnp.dot(x, wg_ref[...], preferred_element_type=F32)
    t = x_ref.shape[0] // nb
    tc = a_ref.shape[1]
    w0, w1, w2 = cw_ref[0:1, :], cw_ref[1:2, :], cw_ref[2:3, :]
    bias = cb_ref[...]
    rowi = lax.broadcasted_iota(jnp.int32, (t, tc), 0)
    for b in range(nb):
        rows = slice(b * t, (b + 1) * t)
        u = zu[rows]
        p0, p1 = prev_ref[b, 0:1, :], prev_ref[b, 1:2, :]
        u1 = jnp.where(rowi == 0, p1, pltpu.roll(u, 1, 0))
        u2 = jnp.where(rowi == 0, p0, jnp.where(rowi == 1, p1, pltpu.roll(u, 2, 0)))
        c = w0 * u2 + w1 * u1 + w2 * u + bias
        a_ref[rows, :] = (zb[rows] * c * _silu(zg[rows])).astype(a_ref.dtype)
        tail_ref[b] = u[t - (CONV_K - 1):, :]


def _conv_branch(x16, wb16, prev, conv_w, conv_b, batch, nb, tc):
    m = x16.shape[0]
    t = m // batch
    wspec = lambda col0: pl.BlockSpec((D_MODEL, tc), lambda c, b: (0, col0 // tc + c))
    return pl.pallas_call(
        functools.partial(_conv_kernel, nb=nb),
        grid=(CONV_WIDTH // tc, batch // nb),
        in_specs=[pl.BlockSpec((nb * t, D_MODEL), lambda c, b: (b, 0)),
                  wspec(WB_CB), wspec(WB_CC), wspec(WB_CH), wspec(WB_CG),
                  pl.BlockSpec((nb, CONV_K - 1, tc), lambda c, b: (b, 0, c)),
                  pl.BlockSpec((CONV_K, tc), lambda c, b: (0, c)),
                  pl.BlockSpec((1, tc), lambda c, b: (0, c))],
        out_specs=[pl.BlockSpec((nb * t, tc), lambda c, b: (b, c)),
                   pl.BlockSpec((nb, CONV_K - 1, tc), lambda c, b: (b, 0, c))],
        out_shape=[jax.ShapeDtypeStruct((m, CONV_WIDTH), BF16),
                   jax.ShapeDtypeStruct((batch, CONV_K - 1, CONV_WIDTH), F32)],
        compiler_params=_params("parallel", "parallel"),
        name="conv_branch",
    )(x16, wb16, wb16, wb16, wb16, prev, conv_w, conv_b)


def _mem_kv_kernel(x_ref, w_ref, o32_ref, o16_ref, *, nb):
    acc = jnp.dot(x_ref[...], w_ref[...], preferred_element_type=F32)
    o16_ref[...] = acc.astype(BF16)
    for b in range(nb):
        for h in range(MEM_HEADS):
            o32_ref[0, b, h] = acc[b * MEM_TOKENS:(b + 1) * MEM_TOKENS,
                                   h * MEM_HEAD_DIM:(h + 1) * MEM_HEAD_DIM]


def _mem_kv(mem16, w16, batch, nb):
    m, k = mem16.shape
    tm = nb * MEM_TOKENS
    return pl.pallas_call(
        functools.partial(_mem_kv_kernel, nb=nb),
        grid=(m // tm, 2),
        in_specs=[pl.BlockSpec((tm, k), lambda i, j: (i, 0)),
                  pl.BlockSpec((k, MEM_WIDTH), lambda i, j: (0, j))],
        out_specs=[pl.BlockSpec((1, nb, MEM_HEADS, MEM_TOKENS, MEM_HEAD_DIM), lambda i, j: (j, i, 0, 0, 0)),
                   pl.BlockSpec((tm, MEM_WIDTH), lambda i, j: (i, j))],
        out_shape=[jax.ShapeDtypeStruct((2, batch, MEM_HEADS, MEM_TOKENS, MEM_HEAD_DIM), F32),
                   jax.ShapeDtypeStruct((m, 2 * MEM_WIDTH), BF16)],
        compiler_params=_params("parallel", "parallel"),
        name="mem_kv",
    )(mem16, w16)


def _mem_attn_kernel(q_ref, k_ref, v_ref, g_ref, o_ref):
    for h in range(MEM_HEADS):
        cs = slice(h * MEM_HEAD_DIM, (h + 1) * MEM_HEAD_DIM)
        s = lax.dot_general(q_ref[:, cs], k_ref[:, cs], _NT, preferred_element_type=F32)
        m = jnp.max(s, axis=-1, keepdims=True)
        p = jnp.exp(s - m)
        l = jnp.sum(p, axis=-1, keepdims=True)
        o = jnp.dot(p.astype(BF16), v_ref[:, cs], preferred_element_type=F32)
        o_ref[:, cs] = (o / l * _silu(g_ref[:, cs])).astype(o_ref.dtype)


def _mem_attn(mq16, mkv16, g_mem, batch, tm):
    m = mq16.shape[0]
    nt = m // batch // tm
    tok = lambda b, i: (b * nt + i, 0)
    return pl.pallas_call(
        _mem_attn_kernel,
        grid=(batch, nt),
        in_specs=[pl.BlockSpec((tm, MEM_WIDTH), tok),
                  pl.BlockSpec((MEM_TOKENS, MEM_WIDTH), lambda b, i: (b, 0)),
                  pl.BlockSpec((MEM_TOKENS, MEM_WIDTH), lambda b, i: (b, 1)),
                  pl.BlockSpec((tm, MEM_WIDTH), tok)],
        out_specs=pl.BlockSpec((tm, MEM_WIDTH), tok),
        out_shape=jax.ShapeDtypeStruct((m, MEM_WIDTH), BF16),
        compiler_params=_params("parallel", "parallel"),
        name="mem_attn",
    )(mq16, mkv16, mkv16, g_mem)


def _merge_kernel(af_ref, ac_ref, am_ref, x_ref, wf_ref, wc_ref, wm_ref,
                  g1w_ref, g2w_ref, g3w_ref, g1b_ref, g2b_ref, g3b_ref, o_ref):
    x = x_ref[...]
    g1 = _sigmoid(jnp.dot(x, g1w_ref[...], preferred_element_type=F32) + g1b_ref[...])
    acc = g1 * jnp.dot(af_ref[...], wf_ref[...], preferred_element_type=F32)
    g2 = _sigmoid(jnp.dot(x, g2w_ref[...], preferred_element_type=F32) + g2b_ref[...])
    acc = acc + g2 * jnp.dot(ac_ref[...], wc_ref[...], preferred_element_type=F32)
    g3 = _sigmoid(jnp.dot(x, g3w_ref[...], preferred_element_type=F32) + g3b_ref[...])
    acc = acc + g3 * jnp.dot(am_ref[...], wm_ref[...], preferred_element_type=F32)
    o_ref[...] = acc.astype(o_ref.dtype)


def _merge(a_fox, a_conv, a_mem, x16, wfo, wco, wmo, wmerge, bmerge, tm, tn):
    m = x16.shape[0]
    nj = D_MODEL // tn
    row = lambda j, i: (i, 0)
    colb = lambda k: (lambda j, i: (0, j + k * nj))
    return pl.pallas_call(
        _merge_kernel,
        grid=(nj, m // tm),
        in_specs=[pl.BlockSpec((tm, FOX_WIDTH), row),
                  pl.BlockSpec((tm, CONV_WIDTH), row),
                  pl.BlockSpec((tm, MEM_WIDTH), row),
                  pl.BlockSpec((tm, D_MODEL), row),
                  pl.BlockSpec((FOX_WIDTH, tn), colb(0)),
                  pl.BlockSpec((CONV_WIDTH, tn), colb(0)),
                  pl.BlockSpec((MEM_WIDTH, tn), colb(0)),
                  pl.BlockSpec((D_MODEL, tn), colb(0)),
                  pl.BlockSpec((D_MODEL, tn), colb(1)),
                  pl.BlockSpec((D_MODEL, tn), colb(2)),
                  pl.BlockSpec((1, tn), colb(0)),
                  pl.BlockSpec((1, tn), colb(1)),
                  pl.BlockSpec((1, tn), colb(2))],
        out_specs=pl.BlockSpec((tm, tn), lambda j, i: (i, j)),
        out_shape=jax.ShapeDtypeStruct((m, D_MODEL), BF16),
        compiler_params=_params("parallel", "parallel"),
        name="merge",
    )(a_fox, a_conv, a_mem, x16, wfo, wco, wmo, wmerge, wmerge, wmerge, bmerge, bmerge, bmerge)


def _out_kernel(m_ref, w_ref, x_ref, g_ref, b_ref, o_ref, *, alpha):
    h = jnp.dot(m_ref[...], w_ref[...], preferred_element_type=F32)
    r = alpha * x_ref[...] + h
    mu = jnp.mean(r, axis=-1, keepdims=True)
    d = r - mu
    var = jnp.mean(d * d, axis=-1, keepdims=True)
    o_ref[...] = d * lax.rsqrt(var + LN_EPS) * g_ref[...] + b_ref[...]


def _out_proj_norm(m16, wo16, x32, ln_g, ln_b, alpha, tm):
    m = x32.shape[0]
    return pl.pallas_call(
        functools.partial(_out_kernel, alpha=alpha),
        grid=(m // tm,),
        in_specs=[pl.BlockSpec((tm, D_MODEL), lambda i: (i, 0)),
                  pl.BlockSpec((D_MODEL, D_MODEL), lambda i: (0, 0)),
                  pl.BlockSpec((tm, D_MODEL), lambda i: (i, 0)),
                  pl.BlockSpec((1, D_MODEL), lambda i: (0, 0)),
                  pl.BlockSpec((1, D_MODEL), lambda i: (0, 0))],
        out_specs=pl.BlockSpec((tm, D_MODEL), lambda i: (i, 0)),
        out_shape=jax.ShapeDtypeStruct((m, D_MODEL), F32),
        compiler_params=_params("parallel"),
        name="out_proj_norm",
    )(m16, wo16, x32, ln_g, ln_b)


def _layer(x32, batch, wts, alpha, mkv16, prev_conv, k_past, v_past, past_logf, tiles):
    tm = tiles["tm"]
    wa, wb = wts["w_a"], wts["w_b"]
    k32, x16 = _project_heads(x32, wa, FOX_WIDTH, batch, tiles["kv_tm"], True, "proj_k")
    v32 = _project_heads(x16, wa, 2 * FOX_WIDTH, batch, tiles["kv_tm"], False, "proj_v")
    q16 = _project(x16, wa, 0, FOX_WIDTH, FOX_HEAD_DIM ** -0.5 * LOG2E, BF16, tm, FOX_WIDTH, "proj_q")
    g_fox = _project(x16, wb, WB_GFOX, FOX_WIDTH, 1.0, F32, tm, FOX_WIDTH, "proj_gfox")
    mq16 = _project(x16, wb, WB_MQ, MEM_WIDTH, MEM_HEAD_DIM ** -0.5, BF16, tm, 512, "proj_mq")
    g_mem = _project(x16, wb, WB_GMEM, MEM_WIDTH, 1.0, F32, tm, 512, "proj_gmem")
    logf, c3 = _fox_prep(x16, wts["w_f"], wts["b_f"], past_logf, batch)
    a_fox = _fox_attn(q16, c3, wts["sel"], wts["sel_ones"], k32, v32, g_fox, k_past, v_past, tiles["tq"])
    a_conv, tail = _conv_branch(x16, wb, prev_conv, wts["conv_w"], wts["conv_b"],
                                batch, tiles["conv_nb"], tiles["conv_tc"])
    a_mem = _mem_attn(mq16, mkv16, g_mem, batch, tiles["mem_tm"])
    m16 = _merge(a_fox, a_conv, a_mem, x16, wts["w_fox_out"], wts["w_conv_out"], wts["w_mem_out"],
                 wts["w_merge"], wts["b_merge"], tiles["merge_tm"], 512)
    y = _out_proj_norm(m16, wts["w_o"], x32, wts["ln_g"], wts["ln_b"], alpha, tiles["out_tm"])
    return y, k32, v32, logf, tail


def _layer_weights(l, w_in, fox_bf, conv_w, conv_b, w_mem_kv, w_fox_out, w_conv_out, w_mem_out,
                   w_merge, b_merge, w_o, ln_g, ln_b):
    w = w_in[l]
    sel, sel_ones = _bias_selectors()
    return {
        "w_a": _wcast(w, 0, OFF_F),
        "w_b": _wcast(w, OFF_B, WB_WIDTH),
        "w_f": jnp.pad(w[:, OFF_F:OFF_B], ((0, 0), (0, LANES - FOX_HEADS))).astype(BF16),
        "b_f": jnp.pad(fox_bf[l].astype(F32), (0, LANES - FOX_HEADS)).reshape(1, LANES),
        "sel": sel,
        "sel_ones": sel_ones,
        "conv_w": conv_w[l],
        "conv_b": conv_b[l].reshape(1, CONV_WIDTH),
        "w_mem_kv": w_mem_kv[l].astype(BF16),
        "w_fox_out": w_fox_out[l].astype(BF16),
        "w_conv_out": w_conv_out[l].astype(BF16),
        "w_mem_out": w_mem_out[l].astype(BF16),
        "w_merge": w_merge[l].astype(BF16),
        "b_merge": b_merge[l].reshape(1, N_BRANCH * D_MODEL),
        "w_o": w_o[l].astype(BF16),
        "ln_g": ln_g[l].reshape(1, D_MODEL),
        "ln_b": ln_b[l].reshape(1, D_MODEL),
    }


def _to_time_major(x):
    return jnp.transpose(x, (0, 2, 1, 3))


def kernel(x_prompt, x_sample, mem_prompt, cache_fox_k, cache_fox_v, cache_fox_logf, state_conv,
           cache_mem_k, cache_mem_v, w_in, fox_bf, conv_w, conv_b, w_mem_kv, w_fox_out, w_conv_out,
           w_mem_out, w_merge, b_merge, w_o, ln_g, ln_b):
    depth = w_in.shape[0]
    alpha = (2 * depth) ** 0.25
    bp, sp, _ = x_prompt.shape
    bs, ts, _ = x_sample.shape
    prompt_tiles = dict(tm=1024, kv_tm=512, tq=256, conv_nb=1, conv_tc=256, mem_tm=1024, merge_tm=512,
                        out_tm=512)
    sample_tiles = dict(tm=bs * ts, kv_tm=bs * ts, tq=ts, conv_nb=bs, conv_tc=256, mem_tm=ts,
                        merge_tm=bs * ts, out_tm=bs * ts)

    hp = x_prompt.reshape(bp * sp, D_MODEL)
    hs = x_sample.reshape(bs * ts, D_MODEL)
    mem16 = mem_prompt.reshape(bp * MEM_TOKENS, D_MODEL).astype(BF16)
    outs = [[] for _ in range(10)]
    for l in range(depth):
        wts = _layer_weights(l, w_in, fox_bf, conv_w, conv_b, w_mem_kv, w_fox_out, w_conv_out,
                             w_mem_out, w_merge, b_merge, w_o, ln_g, ln_b)
        mkv32, mkv16 = _mem_kv(mem16, wts["w_mem_kv"], bp, 4)
        prev0 = jnp.zeros((bp, CONV_K - 1, CONV_WIDTH), F32)
        hp, k_p, v_p, lf_p, tail_p = _layer(hp, bp, wts, alpha, mkv16, prev0, None, None, None,
                                            prompt_tiles)
        ck = jnp.transpose(cache_fox_k[l], (0, 2, 1, 3))
        cv = jnp.transpose(cache_fox_v[l], (0, 2, 1, 3))
        clf = jnp.pad(cache_fox_logf[l].astype(F32), ((0, 0), (0, 0), (0, LANES - FOX_HEADS)))
        cmkv16 = jnp.concatenate([cache_mem_k[l].reshape(bs * MEM_TOKENS, MEM_WIDTH),
                                  cache_mem_v[l].reshape(bs * MEM_TOKENS, MEM_WIDTH)], axis=1).astype(BF16)
        hs, k_s, v_s, lf_s, tail_s = _layer(hs, bs, wts, alpha, cmkv16, state_conv[l].astype(F32),
                                            ck, cv, clf, sample_tiles)
        vals = (_to_time_major(k_p), _to_time_major(v_p),
                lf_p[:, :FOX_HEADS].reshape(bp, sp, FOX_HEADS), tail_p,
                _to_time_major(mkv32[0]), _to_time_major(mkv32[1]),
                _to_time_major(k_s), _to_time_major(v_s),
                lf_s[:, :FOX_HEADS].reshape(bs, ts, FOX_HEADS), tail_s)
        for acc, val in zip(outs, vals):
            acc.append(val)
    return (hp.reshape(bp, sp, D_MODEL), hs.reshape(bs, ts, D_MODEL)) + tuple(jnp.stack(o) for o in outs)
```

```python
import functools
import math

import numpy as np
import jax
import jax.numpy as jnp
from jax import lax
from jax.experimental import pallas as pl
from jax.experimental.pallas import tpu as pltpu

D_MODEL = 2048
FOX_HEADS = 12
FOX_HEAD_DIM = 128
FOX_WIDTH = FOX_HEADS * FOX_HEAD_DIM
CONV_WIDTH = 1536
CONV_K = 3
MEM_TOKENS = 256
MEM_HEADS = 4
MEM_HEAD_DIM = 256
MEM_WIDTH = MEM_HEADS * MEM_HEAD_DIM
N_BRANCH = 3
LN_EPS = 1e-5
NEG_INF = -1e30
LOG2E = math.log2(math.e)

LANES = 128
VMEM_LIMIT = 56 * 1024 * 1024
CUMSUM_BLOCK = 256
F32 = jnp.float32
BF16 = jnp.bfloat16

OFF_F = 3 * FOX_WIDTH
OFF_B = OFF_F + FOX_HEADS
WB_GFOX, WB_CB, WB_CC, WB_CH, WB_CG = (i * FOX_WIDTH for i in range(5))
WB_MQ = 5 * FOX_WIDTH
WB_GMEM = WB_MQ + MEM_WIDTH
WB_WIDTH = WB_GMEM + MEM_WIDTH

_NT = (((1,), (1,)), ((), ()))


def _params(*sem):
    return pltpu.CompilerParams(dimension_semantics=sem, vmem_limit_bytes=VMEM_LIMIT)


def _silu(g):
    return g / (1.0 + jnp.exp(-g))


def _sigmoid(g):
    return 1.0 / (1.0 + jnp.exp(-g))


def _proj_kernel(x_ref, w_ref, o_ref, *, scale):
    acc = lax.dot_general(x_ref[...], w_ref[...], _NT, preferred_element_type=F32)
    if scale != 1.0:
        acc = acc * scale
    o_ref[...] = acc.astype(o_ref.dtype)


def _project(x16, w16, col0, n, scale, out_dtype, tm, tn, name):
    m, k = x16.shape
    assert col0 % tn == 0 and n % tn == 0
    j0 = col0 // tn
    return pl.pallas_call(
        functools.partial(_proj_kernel, scale=scale),
        grid=(m // tm, n // tn),
        in_specs=[pl.BlockSpec((tm, k), lambda i, j: (i, 0)),
                  pl.BlockSpec((tn, k), lambda i, j: (j0 + j, 0))],
        out_specs=pl.BlockSpec((tm, tn), lambda i, j: (i, j)),
        out_shape=jax.ShapeDtypeStruct((m, n), out_dtype),
        compiler_params=_params("parallel", "parallel"),
        name=name,
    )(x16, w16)


def _proj_heads_kernel(x_ref, w_ref, o_ref, *extra_refs, nb, extra):
    x16 = x_ref[...].astype(BF16)
    if extra == "x16":
        extra_refs[0][...] = x16
    acc = lax.dot_general(x16, w_ref[...], _NT, preferred_element_type=F32)
    t = x_ref.shape[0] // nb
    for b in range(nb):
        for h in range(FOX_HEADS):
            tile = acc[b * t:(b + 1) * t, h * LANES:(h + 1) * LANES]
            o_ref[b, h] = tile
            if extra == "transposed":
                extra_refs[0][b, h] = tile.T.astype(BF16)


def _project_heads(x, w16, col0, batch, tm, extra, name):
    m, k = x.shape
    t = m // batch
    j0 = col0 // FOX_WIDTH
    if t >= tm:
        nb, tb, per = 1, tm, t // tm
        o_map = lambda i: (i // per, 0, i % per, 0)
        t_map = lambda i: (i // per, 0, 0, i % per)
    else:
        nb, tb = tm // t, t
        o_map = t_map = lambda i: (i, 0, 0, 0)
    out_specs = [pl.BlockSpec((nb, FOX_HEADS, tb, LANES), o_map)]
    out_shape = [jax.ShapeDtypeStruct((batch, FOX_HEADS, t, LANES), F32)]
    if extra == "x16":
        out_specs.append(pl.BlockSpec((tm, k), lambda i: (i, 0)))
        out_shape.append(jax.ShapeDtypeStruct((m, k), BF16))
    elif extra == "transposed":
        out_specs.append(pl.BlockSpec((nb, FOX_HEADS, LANES, tb), t_map))
        out_shape.append(jax.ShapeDtypeStruct((batch, FOX_HEADS, LANES, t), BF16))
    res = pl.pallas_call(
        functools.partial(_proj_heads_kernel, nb=nb, extra=extra),
        grid=(m // tm,),
        in_specs=[pl.BlockSpec((tm, k), lambda i: (i, 0)),
                  pl.BlockSpec((FOX_WIDTH, k), lambda i: (j0, 0))],
        out_specs=out_specs,
        out_shape=out_shape,
        compiler_params=_params("parallel"),
        name=name,
    )(x, w16)
    return res if extra else res[0]


def _split3(x):
    hi = x.astype(BF16)
    rem = x - hi.astype(F32)
    mid = rem.astype(BF16)
    lo = (rem - mid.astype(F32)).astype(BF16)
    return hi, mid, lo


def _cumsum_rows(x):
    s = x.shape[0]
    r = lax.broadcasted_iota(jnp.int32, (CUMSUM_BLOCK, CUMSUM_BLOCK), 0)
    c = lax.broadcasted_iota(jnp.int32, (CUMSUM_BLOCK, CUMSUM_BLOCK), 1)
    tri = (r >= c).astype(BF16)
    carry = jnp.zeros((1, x.shape[1]), F32)
    out = []
    for start in range(0, s, CUMSUM_BLOCK):
        n = min(CUMSUM_BLOCK, s - start)
        hi, mid, lo = _split3(x[start:start + n])
        t = tri[:n, :n]
        cb = (jnp.dot(t, hi, preferred_element_type=F32)
              + jnp.dot(t, mid, preferred_element_type=F32)
              + jnp.dot(t, lo, preferred_element_type=F32)) + carry
        carry = cb[n - 1:n]
        out.append(cb)
    return jnp.concatenate(out, axis=0) if len(out) > 1 else out[0]


def _fox_prep_kernel(*refs, past_len):
    if past_len:
        x_ref, wf_ref, bf_ref, past_ref, logf_ref, c3_ref = refs
    else:
        x_ref, wf_ref, bf_ref, logf_ref, c3_ref = refs
    z = lax.dot_general(x_ref[...], wf_ref[...], _NT, preferred_element_type=F32) + bf_ref[...]
    logf = jnp.minimum(z, 0.0) - jnp.log1p(jnp.exp(-jnp.abs(z)))
    logf_ref[...] = logf
    full = jnp.concatenate([past_ref[0], logf], axis=0) if past_len else logf
    lane = lax.broadcasted_iota(jnp.int32, (1, LANES), 1)
    cum = jnp.where(lane < FOX_HEADS, _cumsum_rows(full) * LOG2E, 0.0)
    hi, mid, lo = (p.astype(F32) for p in _split3(cum))
    packed = hi + pltpu.roll(mid, FOX_HEADS, 1) + pltpu.roll(lo, 2 * FOX_HEADS, 1)
    c3_ref[...] = packed.astype(BF16)


def _fox_prep(x16, wf16, bf_pad, past_logf, batch):
    m = x16.shape[0]
    t = m // batch
    past_len = 0 if past_logf is None else past_logf.shape[1]
    sk = past_len + t
    in_specs = [pl.BlockSpec((t, D_MODEL), lambda b: (b, 0)),
                pl.BlockSpec((LANES, D_MODEL), lambda b: (0, 0)),
                pl.BlockSpec((1, LANES), lambda b: (0, 0))]
    args = [x16, wf16, bf_pad]
    if past_len:
        in_specs.append(pl.BlockSpec((1, past_len, LANES), lambda b: (b, 0, 0)))
        args.append(past_logf)
    return pl.pallas_call(
        functools.partial(_fox_prep_kernel, past_len=past_len),
        grid=(batch,),
        in_specs=in_specs,
        out_specs=[pl.BlockSpec((t, LANES), lambda b: (b, 0)),
                   pl.BlockSpec((sk, LANES), lambda b: (b, 0))],
        out_shape=[jax.ShapeDtypeStruct((m, LANES), F32),
                   jax.ShapeDtypeStruct((batch * sk, LANES), BF16)],
        compiler_params=_params("parallel"),
        name="fox_prep",
    )(*args)


def _bias_selectors():
    sel = np.zeros((FOX_HEADS, LANES, 2 * LANES), np.float32)
    for h in range(FOX_HEADS):
        for piece in range(3):
            sel[h, piece * FOX_HEADS + h, piece] = 1.0
            sel[h, piece * FOX_HEADS + h, LANES + 3 + piece] = -1.0
    ones = np.zeros((2, 1, LANES), np.float32)
    ones[0, 0, 3:6] = 1.0
    ones[1, 0, 0:3] = 1.0
    return jnp.asarray(sel, BF16), jnp.asarray(ones, F32)


def _fill_qk(q_ref, c3_ref, sel_ref, one_ref, k_ref, kp_ref, qq, kk, past_len):
    aug = jnp.dot(c3_ref[...], sel_ref[0], preferred_element_type=F32)
    qq[:, :LANES] = q_ref[...]
    qq[:, LANES:] = (aug[past_len:, :LANES] + one_ref[0]).astype(BF16)
    if past_len:
        kk[:past_len, :LANES] = kp_ref[0, 0].astype(BF16)
    kk[past_len:, :LANES] = k_ref[0, 0].astype(BF16)
    kk[:, LANES:] = (aug[:, LANES:] + one_ref[1]).astype(BF16)


def _fox_attn_t_kernel(q_ref, c3_ref, sel_ref, one_ref, k_ref, vt_ref, g_ref, o_ref, qq, kk, *, tq):
    _fill_qk(q_ref, c3_ref, sel_ref, one_ref, k_ref, None, qq, kk, 0)
    t = q_ref.shape[0]
    key = lax.broadcasted_iota(jnp.int32, (tq, tq), 0)
    qry = lax.broadcasted_iota(jnp.int32, (tq, tq), 1)
    causal = key <= qry
    def scores(r0):
        qi = qq[r0:r0 + tq, :]
        s_d = lax.dot_general(kk[r0:r0 + tq, :], qi, _NT, preferred_element_type=F32)
        s_p = lax.dot_general(kk[:r0, :], qi, _NT, preferred_element_type=F32) if r0 else None
        return jnp.where(causal, s_d, NEG_INF), s_p

    ahead = scores(0)
    for i in range(t // tq):
        r0 = i * tq
        s_d, s_p = ahead
        if r0 + tq < t:
            ahead = scores(r0 + tq)
        m = jnp.max(s_d, axis=0, keepdims=True)
        if r0:
            m = jnp.maximum(m, jnp.max(s_p, axis=0, keepdims=True))
            p_p = jnp.exp2(s_p - m)
        p_d = jnp.exp2(s_d - m)
        l = jnp.sum(p_d, axis=0, keepdims=True)
        o = jnp.dot(vt_ref[0, 0, :, r0:r0 + tq], p_d.astype(BF16), preferred_element_type=F32)
        if r0:
            l = l + jnp.sum(p_p, axis=0, keepdims=True)
            o = o + jnp.dot(vt_ref[0, 0, :, :r0], p_p.astype(BF16), preferred_element_type=F32)
        g = g_ref[r0:r0 + tq, :]
        o_ref[r0:r0 + tq, :] = ((o / l).T * _silu(g)).astype(o_ref.dtype)


def _fox_attn_t(q16, c3, sel, ones, k32, vt16, g_fox, tq):
    batch, _, t, _ = k32.shape
    tok = lambda b, h: (b, h)
    per_head = lambda b, h: (b, h, 0, 0)
    return pl.pallas_call(
        functools.partial(_fox_attn_t_kernel, tq=tq),
        grid=(batch, FOX_HEADS),
        in_specs=[pl.BlockSpec((t, LANES), tok),
                  pl.BlockSpec((t, LANES), lambda b, h: (b, 0)),
                  pl.BlockSpec((1, LANES, 2 * LANES), lambda b, h: (h, 0, 0)),
                  pl.BlockSpec((2, 1, LANES), lambda b, h: (0, 0, 0)),
                  pl.BlockSpec((1, 1, t, LANES), per_head),
                  pl.BlockSpec((1, 1, LANES, t), per_head),
                  pl.BlockSpec((t, LANES), tok)],
        out_specs=pl.BlockSpec((t, LANES), tok),
        out_shape=jax.ShapeDtypeStruct((batch * t, FOX_WIDTH), BF16),
        scratch_shapes=[pltpu.VMEM((t, 2 * LANES), BF16),
                        pltpu.VMEM((t, 2 * LANES), BF16)],
        compiler_params=_params("parallel", "parallel"),
        name="fox_attn_t",
    )(q16, c3, sel, ones, k32, vt16, g_fox)


def _fox_attn_kernel(*refs, past_len, tq):
    if past_len:
        q_ref, c3_ref, sel_ref, one_ref, k_ref, v_ref, g_ref, kp_ref, vp_ref, o_ref, qq, kk, vv = refs
        vv[:past_len, :] = vp_ref[0, 0].astype(BF16)
    else:
        q_ref, c3_ref, sel_ref, one_ref, k_ref, v_ref, g_ref, o_ref, qq, kk, vv = refs
        kp_ref = None
    t = q_ref.shape[0]
    _fill_qk(q_ref, c3_ref, sel_ref, one_ref, k_ref, kp_ref, qq, kk, past_len)
    vv[past_len:, :] = v_ref[0, 0].astype(BF16)

    row = lax.broadcasted_iota(jnp.int32, (tq, tq), 0)
    col = lax.broadcasted_iota(jnp.int32, (tq, tq), 1)
    causal = col <= row
    for i in range(t // tq):
        r0 = i * tq
        kv_len = past_len + r0
        qi = qq[r0:r0 + tq, :]
        s_d = lax.dot_general(qi, kk[kv_len:kv_len + tq, :], _NT, preferred_element_type=F32)
        s_d = jnp.where(causal, s_d, NEG_INF)
        m = jnp.max(s_d, axis=-1, keepdims=True)
        if kv_len:
            s_p = lax.dot_general(qi, kk[:kv_len, :], _NT, preferred_element_type=F32)
            m = jnp.maximum(m, jnp.max(s_p, axis=-1, keepdims=True))
            p_p = jnp.exp2(s_p - m)
        p_d = jnp.exp2(s_d - m)
        l = jnp.sum(p_d, axis=-1, keepdims=True)
        o = jnp.dot(p_d.astype(BF16), vv[kv_len:kv_len + tq, :], preferred_element_type=F32)
        if kv_len:
            l = l + jnp.sum(p_p, axis=-1, keepdims=True)
            o = o + jnp.dot(p_p.astype(BF16), vv[:kv_len, :], preferred_element_type=F32)
        g = g_ref[r0:r0 + tq, :]
        o_ref[r0:r0 + tq, :] = (o / l * _silu(g)).astype(o_ref.dtype)


def _fox_attn(q16, c3, sel, ones, k32, v32, g_fox, k_past, v_past, tq):
    batch, _, t, _ = k32.shape
    m = batch * t
    past_len = 0 if k_past is None else k_past.shape[2]
    sk = past_len + t
    tok = lambda b, h: (b, h)
    per_head = lambda b, h: (b, h, 0, 0)
    in_specs = [pl.BlockSpec((t, LANES), tok),
                pl.BlockSpec((sk, LANES), lambda b, h: (b, 0)),
                pl.BlockSpec((1, LANES, 2 * LANES), lambda b, h: (h, 0, 0)),
                pl.BlockSpec((2, 1, LANES), lambda b, h: (0, 0, 0)),
                pl.BlockSpec((1, 1, t, LANES), per_head),
                pl.BlockSpec((1, 1, t, LANES), per_head),
                pl.BlockSpec((t, LANES), tok)]
    args = [q16, c3, sel, ones, k32, v32, g_fox]
    if past_len:
        in_specs += [pl.BlockSpec((1, 1, past_len, LANES), per_head)] * 2
        args += [k_past, v_past]
    return pl.pallas_call(
        functools.partial(_fox_attn_kernel, past_len=past_len, tq=tq),
        grid=(batch, FOX_HEADS),
        in_specs=in_specs,
        out_specs=pl.BlockSpec((t, LANES), tok),
        out_shape=jax.ShapeDtypeStruct((m, FOX_WIDTH), BF16),
        scratch_shapes=[pltpu.VMEM((t, 2 * LANES), BF16),
                        pltpu.VMEM((sk, 2 * LANES), BF16),
                        pltpu.VMEM((sk, LANES), BF16)],
        compiler_params=_params("parallel", "parallel"),
        name="fox_attn",
    )(*args)


def _conv_kernel(x_ref, wb_ref, wc_ref, wh_ref, wg_ref, prev_ref, cw_ref, cb_ref, a_ref, tail_ref, *, nb):
    x = x_ref[...]
    zb = lax.dot_general(x, wb_ref[...], _NT, preferred_element_type=F32)
    zu = (lax.dot_general(x, wc_ref[...], _NT, preferred_element_type=F32)
          * lax.dot_general(x, wh_ref[...], _NT, preferred_element_type=F32))
    zg = lax.dot_general(x, wg_ref[...], _NT, preferred_element_type=F32)
    t = x_ref.shape[0] // nb
    tc = a_ref.shape[1]
    w0, w1, w2 = cw_ref[0:1, :], cw_ref[1:2, :], cw_ref[2:3, :]
    bias = cb_ref[...]
    rowi = lax.broadcasted_iota(jnp.int32, (t, tc), 0)
    for b in range(nb):
        rows = slice(b * t, (b + 1) * t)
        u = zu[rows]
        p0, p1 = prev_ref[b, 0:1, :], prev_ref[b, 1:2, :]
        u1 = jnp.where(rowi == 0, p1, pltpu.roll(u, 1, 0))
        u2 = jnp.where(rowi == 0, p0, jnp.where(rowi == 1, p1, pltpu.roll(u, 2, 0)))
        c = w0 * u2 + w1 * u1 + w2 * u + bias
        a_ref[rows, :] = (zb[rows] * c * _silu(zg[rows])).astype(a_ref.dtype)
        tail_ref[b] = u[t - (CONV_K - 1):, :]


def _conv_branch(x16, wb16, prev, conv_w, conv_b, batch, nb, tc):
    m = x16.shape[0]
    t = m // batch
    wspec = lambda col0: pl.BlockSpec((tc, D_MODEL), lambda c, b: (col0 // tc + c, 0))
    return pl.pallas_call(
        functools.partial(_conv_kernel, nb=nb),
        grid=(CONV_WIDTH // tc, batch // nb),
        in_specs=[pl.BlockSpec((nb * t, D_MODEL), lambda c, b: (b, 0)),
                  wspec(WB_CB), wspec(WB_CC), wspec(WB_CH), wspec(WB_CG),
                  pl.BlockSpec((nb, CONV_K - 1, tc), lambda c, b: (b, 0, c)),
                  pl.BlockSpec((CONV_K, tc), lambda c, b: (0, c)),
                  pl.BlockSpec((1, tc), lambda c, b: (0, c))],
        out_specs=[pl.BlockSpec((nb * t, tc), lambda c, b: (b, c)),
                   pl.BlockSpec((nb, CONV_K - 1, tc), lambda c, b: (b, 0, c))],
        out_shape=[jax.ShapeDtypeStruct((m, CONV_WIDTH), BF16),
                   jax.ShapeDtypeStruct((batch, CONV_K - 1, CONV_WIDTH), F32)],
        compiler_params=_params("parallel", "parallel"),
        name="conv_branch",
    )(x16, wb16, wb16, wb16, wb16, prev, conv_w, conv_b)


def _mem_kv_kernel(x_ref, w_ref, o32_ref, o16_ref, *, nb):
    acc = jnp.dot(x_ref[...], w_ref[...], preferred_element_type=F32)
    o16_ref[...] = acc.astype(BF16)
    for b in range(nb):
        for h in range(MEM_HEADS):
            o32_ref[0, b, h] = acc[b * MEM_TOKENS:(b + 1) * MEM_TOKENS,
                                   h * MEM_HEAD_DIM:(h + 1) * MEM_HEAD_DIM]


def _mem_kv(mem16, w16, batch, nb):
    m, k = mem16.shape
    tm = nb * MEM_TOKENS
    return pl.pallas_call(
        functools.partial(_mem_kv_kernel, nb=nb),
        grid=(m // tm, 2),
        in_specs=[pl.BlockSpec((tm, k), lambda i, j: (i, 0)),
                  pl.BlockSpec((k, MEM_WIDTH), lambda i, j: (0, j))],
        out_specs=[pl.BlockSpec((1, nb, MEM_HEADS, MEM_TOKENS, MEM_HEAD_DIM), lambda i, j: (j, i, 0, 0, 0)),
                   pl.BlockSpec((tm, MEM_WIDTH), lambda i, j: (i, j))],
        out_shape=[jax.ShapeDtypeStruct((2, batch, MEM_HEADS, MEM_TOKENS, MEM_HEAD_DIM), F32),
                   jax.ShapeDtypeStruct((m, 2 * MEM_WIDTH), BF16)],
        compiler_params=_params("parallel", "parallel"),
        name="mem_kv",
    )(mem16, w16)


def _mem_attn_kernel(q_ref, k_ref, v_ref, g_ref, o_ref):
    for h in range(MEM_HEADS):
        cs = slice(h * MEM_HEAD_DIM, (h + 1) * MEM_HEAD_DIM)
        s = lax.dot_general(q_ref[:, cs], k_ref[:, cs], _NT, preferred_element_type=F32)
        m = jnp.max(s, axis=-1, keepdims=True)
        p = jnp.exp(s - m)
        l = jnp.sum(p, axis=-1, keepdims=True)
        o = jnp.dot(p.astype(BF16), v_ref[:, cs], preferred_element_type=F32)
        o_ref[:, cs] = (o / l * _silu(g_ref[:, cs])).astype(o_ref.dtype)


def _mem_attn(mq16, mkv16, g_mem, batch, tm):
    m = mq16.shape[0]
    nt = m // batch // tm
    tok = lambda b, i: (b * nt + i, 0)
    return pl.pallas_call(
        _mem_attn_kernel,
        grid=(batch, nt),
        in_specs=[pl.BlockSpec((tm, MEM_WIDTH), tok),
                  pl.BlockSpec((MEM_TOKENS, MEM_WIDTH), lambda b, i: (b, 0)),
                  pl.BlockSpec((MEM_TOKENS, MEM_WIDTH), lambda b, i: (b, 1)),
                  pl.BlockSpec((tm, MEM_WIDTH), tok)],
        out_specs=pl.BlockSpec((tm, MEM_WIDTH), tok),
        out_shape=jax.ShapeDtypeStruct((m, MEM_WIDTH), BF16),
        compiler_params=_params("parallel", "parallel"),
        name="mem_attn",
    )(mq16, mkv16, mkv16, g_mem)


def _merge_kernel(af_ref, ac_ref, am_ref, x_ref, wf_ref, wc_ref, wm_ref,
                  g1w_ref, g2w_ref, g3w_ref, g1b_ref, g2b_ref, g3b_ref, o_ref):
    x = x_ref[...]
    g1 = _sigmoid(jnp.dot(x, g1w_ref[...], preferred_element_type=F32) + g1b_ref[...])
    acc = g1 * jnp.dot(af_ref[...], wf_ref[...], preferred_element_type=F32)
    g2 = _sigmoid(jnp.dot(x, g2w_ref[...], preferred_element_type=F32) + g2b_ref[...])
    acc = acc + g2 * jnp.dot(ac_ref[...], wc_ref[...], preferred_element_type=F32)
    g3 = _sigmoid(jnp.dot(x, g3w_ref[...], preferred_element_type=F32) + g3b_ref[...])
    acc = acc + g3 * jnp.dot(am_ref[...], wm_ref[...], preferred_element_type=F32)
    o_ref[...] = acc.astype(o_ref.dtype)


def _merge(a_fox, a_conv, a_mem, x16, wfo, wco, wmo, wmerge, bmerge, tm, tn):
    m = x16.shape[0]
    nj = D_MODEL // tn
    row = lambda j, i: (i, 0)
    colb = lambda k: (lambda j, i: (0, j + k * nj))
    return pl.pallas_call(
        _merge_kernel,
        grid=(nj, m // tm),
        in_specs=[pl.BlockSpec((tm, FOX_WIDTH), row),
                  pl.BlockSpec((tm, CONV_WIDTH), row),
                  pl.BlockSpec((tm, MEM_WIDTH), row),
                  pl.BlockSpec((tm, D_MODEL), row),
                  pl.BlockSpec((FOX_WIDTH, tn), colb(0)),
                  pl.BlockSpec((CONV_WIDTH, tn), colb(0)),
                  pl.BlockSpec((MEM_WIDTH, tn), colb(0)),
                  pl.BlockSpec((D_MODEL, tn), colb(0)),
                  pl.BlockSpec((D_MODEL, tn), colb(1)),
                  pl.BlockSpec((D_MODEL, tn), colb(2)),
                  pl.BlockSpec((1, tn), colb(0)),
                  pl.BlockSpec((1, tn), colb(1)),
                  pl.BlockSpec((1, tn), colb(2))],
        out_specs=pl.BlockSpec((tm, tn), lambda j, i: (i, j)),
        out_shape=jax.ShapeDtypeStruct((m, D_MODEL), BF16),
        compiler_params=_params("parallel", "parallel"),
        name="merge",
    )(a_fox, a_conv, a_mem, x16, wfo, wco, wmo, wmerge, wmerge, wmerge, bmerge, bmerge, bmerge)


def _out_kernel(m_ref, w_ref, x_ref, g_ref, b_ref, o_ref, *, alpha):
    h = jnp.dot(m_ref[...], w_ref[...], preferred_element_type=F32)
    r = alpha * x_ref[...] + h
    mu = jnp.mean(r, axis=-1, keepdims=True)
    d = r - mu
    var = jnp.mean(d * d, axis=-1, keepdims=True)
    o_ref[...] = d * lax.rsqrt(var + LN_EPS) * g_ref[...] + b_ref[...]


def _out_proj_norm(m16, wo16, x32, ln_g, ln_b, alpha, tm):
    m = x32.shape[0]
    return pl.pallas_call(
        functools.partial(_out_kernel, alpha=alpha),
        grid=(m // tm,),
        in_specs=[pl.BlockSpec((tm, D_MODEL), lambda i: (i, 0)),
                  pl.BlockSpec((D_MODEL, D_MODEL), lambda i: (0, 0)),
                  pl.BlockSpec((tm, D_MODEL), lambda i: (i, 0)),
                  pl.BlockSpec((1, D_MODEL), lambda i: (0, 0)),
                  pl.BlockSpec((1, D_MODEL), lambda i: (0, 0))],
        out_specs=pl.BlockSpec((tm, D_MODEL), lambda i: (i, 0)),
        out_shape=jax.ShapeDtypeStruct((m, D_MODEL), F32),
        compiler_params=_params("parallel"),
        name="out_proj_norm",
    )(m16, wo16, x32, ln_g, ln_b)


def _layer(x32, batch, wts, alpha, mkv16, prev_conv, k_past, v_past, past_logf, tiles):
    tm = tiles["tm"]
    wa, wb = wts["w_a"], wts["w_b"]
    k32, x16 = _project_heads(x32, wa, FOX_WIDTH, batch, tiles["kv_tm"], "x16", "proj_k")
    if k_past is None:
        v32, vt16 = _project_heads(x16, wa, 2 * FOX_WIDTH, batch, tiles["kv_tm"], "transposed", "proj_v")
    else:
        v32 = _project_heads(x16, wa, 2 * FOX_WIDTH, batch, tiles["kv_tm"], None, "proj_v")
    q16 = _project(x16, wa, 0, FOX_WIDTH, FOX_HEAD_DIM ** -0.5 * LOG2E, BF16, tm, FOX_WIDTH, "proj_q")
    g_fox = _project(x16, wb, WB_GFOX, FOX_WIDTH, 1.0, F32, tm, FOX_WIDTH, "proj_gfox")
    mq16 = _project(x16, wb, WB_MQ, MEM_WIDTH, MEM_HEAD_DIM ** -0.5, BF16, tm, 512, "proj_mq")
    g_mem = _project(x16, wb, WB_GMEM, MEM_WIDTH, 1.0, F32, tm, 512, "proj_gmem")
    logf, c3 = _fox_prep(x16, wts["w_f"], wts["b_f"], past_logf, batch)
    if k_past is None:
        a_fox = _fox_attn_t(q16, c3, wts["sel"], wts["sel_ones"], k32, vt16, g_fox, tiles["tq"])
    else:
        a_fox = _fox_attn(q16, c3, wts["sel"], wts["sel_ones"], k32, v32, g_fox, k_past, v_past,
                          tiles["tq"])
    a_conv, tail = _conv_branch(x16, wb, prev_conv, wts["conv_w"], wts["conv_b"],
                                batch, tiles["conv_nb"], tiles["conv_tc"])
    a_mem = _mem_attn(mq16, mkv16, g_mem, batch, tiles["mem_tm"])
    m16 = _merge(a_fox, a_conv, a_mem, x16, wts["w_fox_out"], wts["w_conv_out"], wts["w_mem_out"],
                 wts["w_merge"], wts["b_merge"], tiles["merge_tm"], 512)
    y = _out_proj_norm(m16, wts["w_o"], x32, wts["ln_g"], wts["ln_b"], alpha, tiles["out_tm"])
    return y, k32, v32, logf, tail


def _layer_weights(l, w_in, fox_bf, conv_w, conv_b, w_mem_kv, w_fox_out, w_conv_out, w_mem_out,
                   w_merge, b_merge, w_o, ln_g, ln_b):
    w = w_in[l]
    sel, sel_ones = _bias_selectors()
    return {
        "w_a": w[:, :OFF_F].T.astype(BF16),
        "w_b": w[:, OFF_B:].T.astype(BF16),
        "w_f": jnp.pad(w[:, OFF_F:OFF_B].T, ((0, LANES - FOX_HEADS), (0, 0))).astype(BF16),
        "b_f": jnp.pad(fox_bf[l].astype(F32), (0, LANES - FOX_HEADS)).reshape(1, LANES),
        "sel": sel,
        "sel_ones": sel_ones,
        "conv_w": conv_w[l],
        "conv_b": conv_b[l].reshape(1, CONV_WIDTH),
        "w_mem_kv": w_mem_kv[l].astype(BF16),
        "w_fox_out": w_fox_out[l].astype(BF16),
        "w_conv_out": w_conv_out[l].astype(BF16),
        "w_mem_out": w_mem_out[l].astype(BF16),
        "w_merge": w_merge[l].astype(BF16),
        "b_merge": b_merge[l].reshape(1, N_BRANCH * D_MODEL),
        "w_o": w_o[l].astype(BF16),
        "ln_g": ln_g[l].reshape(1, D_MODEL),
        "ln_b": ln_b[l].reshape(1, D_MODEL),
    }


def _to_time_major(x):
    return jnp.transpose(x, (0, 2, 1, 3))


def kernel(x_prompt, x_sample, mem_prompt, cache_fox_k, cache_fox_v, cache_fox_logf, state_conv,
           cache_mem_k, cache_mem_v, w_in, fox_bf, conv_w, conv_b, w_mem_kv, w_fox_out, w_conv_out,
           w_mem_out, w_merge, b_merge, w_o, ln_g, ln_b):
    depth = w_in.shape[0]
    alpha = (2 * depth) ** 0.25
    bp, sp, _ = x_prompt.shape
    bs, ts, _ = x_sample.shape
    prompt_tiles = dict(tm=1024, kv_tm=512, tq=256, conv_nb=1, conv_tc=256, mem_tm=1024, merge_tm=512,
                        out_tm=512)
    sample_tiles = dict(tm=bs * ts, kv_tm=bs * ts, tq=ts, conv_nb=bs, conv_tc=256, mem_tm=ts,
                        merge_tm=bs * ts, out_tm=bs * ts)

    hp = x_prompt.reshape(bp * sp, D_MODEL)
    hs = x_sample.reshape(bs * ts, D_MODEL)
    mem16 = mem_prompt.reshape(bp * MEM_TOKENS, D_MODEL).astype(BF16)
    outs = [[] for _ in range(10)]
    for l in range(depth):
        wts = _layer_weights(l, w_in, fox_bf, conv_w, conv_b, w_mem_kv, w_fox_out, w_conv_out,
                             w_mem_out, w_merge, b_merge, w_o, ln_g, ln_b)
        mkv32, mkv16 = _mem_kv(mem16, wts["w_mem_kv"], bp, 4)
        prev0 = jnp.zeros((bp, CONV_K - 1, CONV_WIDTH), F32)
        hp, k_p, v_p, lf_p, tail_p = _layer(hp, bp, wts, alpha, mkv16, prev0, None, None, None,
                                            prompt_tiles)
        ck = jnp.transpose(cache_fox_k[l], (0, 2, 1, 3))
        cv = jnp.transpose(cache_fox_v[l], (0, 2, 1, 3))
        clf = jnp.pad(cache_fox_logf[l].astype(F32), ((0, 0), (0, 0), (0, LANES - FOX_HEADS)))
        cmkv16 = jnp.concatenate([cache_mem_k[l].reshape(bs * MEM_TOKENS, MEM_WIDTH),
                                  cache_mem_v[l].reshape(bs * MEM_TOKENS, MEM_WIDTH)], axis=1).astype(BF16)
        hs, k_s, v_s, lf_s, tail_s = _layer(hs, bs, wts, alpha, cmkv16, state_conv[l].astype(F32),
                                            ck, cv, clf, sample_tiles)
        vals = (_to_time_major(k_p), _to_time_major(v_p),
                lf_p[:, :FOX_HEADS].reshape(bp, sp, FOX_HEADS), tail_p,
                _to_time_major(mkv32[0]), _to_time_major(mkv32[1]),
                _to_time_major(k_s), _to_time_major(v_s),
                lf_s[:, :FOX_HEADS].reshape(bs, ts, FOX_HEADS), tail_s)
        for acc, val in zip(outs, vals):
            acc.append(val)
    return (hp.reshape(bp, sp, D_MODEL), hs.reshape(bs, ts, D_MODEL)) + tuple(jnp.stack(o) for o in outs)
```

```python
import functools
import math

import numpy as np
import jax
import jax.numpy as jnp
from jax import lax
from jax.experimental import pallas as pl
from jax.experimental.pallas import tpu as pltpu

D_MODEL = 2048
FOX_HEADS = 12
FOX_HEAD_DIM = 128
FOX_WIDTH = FOX_HEADS * FOX_HEAD_DIM
CONV_WIDTH = 1536
CONV_K = 3
MEM_TOKENS = 256
MEM_HEADS = 4
MEM_HEAD_DIM = 256
MEM_WIDTH = MEM_HEADS * MEM_HEAD_DIM
N_BRANCH = 3
LN_EPS = 1e-5
NEG_INF = -1e30
LOG2E = math.log2(math.e)

LANES = 128
VMEM_LIMIT = 56 * 1024 * 1024
CUMSUM_BLOCK = 256
CONV_ROW_GROUP = 512
F32 = jnp.float32
BF16 = jnp.bfloat16

OFF_F = 3 * FOX_WIDTH
OFF_B = OFF_F + FOX_HEADS
WB_GFOX, WB_CB, WB_CC, WB_CH, WB_CG = (i * FOX_WIDTH for i in range(5))
WB_MQ = 5 * FOX_WIDTH
WB_GMEM = WB_MQ + MEM_WIDTH
WB_WIDTH = WB_GMEM + MEM_WIDTH

_NT = (((1,), (1,)), ((), ()))


def _params(*sem):
    return pltpu.CompilerParams(dimension_semantics=sem, vmem_limit_bytes=VMEM_LIMIT)


def _silu(g):
    return g / (1.0 + jnp.exp(-g))


def _sigmoid(g):
    return 1.0 / (1.0 + jnp.exp(-g))


def _proj_kernel(x_ref, w_ref, o_ref, *, scale):
    acc = lax.dot_general(x_ref[...], w_ref[...], _NT, preferred_element_type=F32)
    if scale != 1.0:
        acc = acc * scale
    o_ref[...] = acc.astype(o_ref.dtype)


def _project(x16, w16, col0, n, scale, out_dtype, tm, tn, name):
    m, k = x16.shape
    assert col0 % tn == 0 and n % tn == 0
    j0 = col0 // tn
    return pl.pallas_call(
        functools.partial(_proj_kernel, scale=scale),
        grid=(m // tm, n // tn),
        in_specs=[pl.BlockSpec((tm, k), lambda i, j: (i, 0)),
                  pl.BlockSpec((tn, k), lambda i, j: (j0 + j, 0))],
        out_specs=pl.BlockSpec((tm, tn), lambda i, j: (i, j)),
        out_shape=jax.ShapeDtypeStruct((m, n), out_dtype),
        compiler_params=_params("parallel", "parallel"),
        name=name,
    )(x16, w16)


def _proj_heads_kernel(x_ref, w_ref, o_ref, *extra_refs, nb, extra):
    x16 = x_ref[...].astype(BF16)
    if extra == "x16":
        extra_refs[0][...] = x16
    acc = lax.dot_general(x16, w_ref[...], _NT, preferred_element_type=F32)
    t = x_ref.shape[0] // nb
    for b in range(nb):
        for h in range(FOX_HEADS):
            tile = acc[b * t:(b + 1) * t, h * LANES:(h + 1) * LANES]
            o_ref[b, h] = tile
            if extra == "transposed":
                extra_refs[0][b, h] = tile.T.astype(BF16)


def _project_heads(x, w16, col0, batch, tm, extra, name):
    m, k = x.shape
    t = m // batch
    j0 = col0 // FOX_WIDTH
    if t >= tm:
        nb, tb, per = 1, tm, t // tm
        o_map = lambda i: (i // per, 0, i % per, 0)
        t_map = lambda i: (i // per, 0, 0, i % per)
    else:
        nb, tb = tm // t, t
        o_map = t_map = lambda i: (i, 0, 0, 0)
    out_specs = [pl.BlockSpec((nb, FOX_HEADS, tb, LANES), o_map)]
    out_shape = [jax.ShapeDtypeStruct((batch, FOX_HEADS, t, LANES), F32)]
    if extra == "x16":
        out_specs.append(pl.BlockSpec((tm, k), lambda i: (i, 0)))
        out_shape.append(jax.ShapeDtypeStruct((m, k), BF16))
    elif extra == "transposed":
        out_specs.append(pl.BlockSpec((nb, FOX_HEADS, LANES, tb), t_map))
        out_shape.append(jax.ShapeDtypeStruct((batch, FOX_HEADS, LANES, t), BF16))
    res = pl.pallas_call(
        functools.partial(_proj_heads_kernel, nb=nb, extra=extra),
        grid=(m // tm,),
        in_specs=[pl.BlockSpec((tm, k), lambda i: (i, 0)),
                  pl.BlockSpec((FOX_WIDTH, k), lambda i: (j0, 0))],
        out_specs=out_specs,
        out_shape=out_shape,
        compiler_params=_params("parallel"),
        name=name,
    )(x, w16)
    return res if extra else res[0]


def _split3(x):
    hi = x.astype(BF16)
    rem = x - hi.astype(F32)
    mid = rem.astype(BF16)
    lo = (rem - mid.astype(F32)).astype(BF16)
    return hi, mid, lo


def _cumsum_rows(x):
    s = x.shape[0]
    r = lax.broadcasted_iota(jnp.int32, (CUMSUM_BLOCK, CUMSUM_BLOCK), 0)
    c = lax.broadcasted_iota(jnp.int32, (CUMSUM_BLOCK, CUMSUM_BLOCK), 1)
    tri = (r >= c).astype(BF16)
    carry = jnp.zeros((1, x.shape[1]), F32)
    out = []
    for start in range(0, s, CUMSUM_BLOCK):
        n = min(CUMSUM_BLOCK, s - start)
        hi, mid, lo = _split3(x[start:start + n])
        t = tri[:n, :n]
        cb = (jnp.dot(t, hi, preferred_element_type=F32)
              + jnp.dot(t, mid, preferred_element_type=F32)
              + jnp.dot(t, lo, preferred_element_type=F32)) + carry
        carry = cb[n - 1:n]
        out.append(cb)
    return jnp.concatenate(out, axis=0) if len(out) > 1 else out[0]


def _fox_prep_kernel(*refs, past_len):
    if past_len:
        x_ref, wf_ref, bf_ref, past_ref, logf_ref, c3_ref = refs
    else:
        x_ref, wf_ref, bf_ref, logf_ref, c3_ref = refs
    z = lax.dot_general(x_ref[...], wf_ref[...], _NT, preferred_element_type=F32) + bf_ref[...]
    logf = jnp.minimum(z, 0.0) - jnp.log1p(jnp.exp(-jnp.abs(z)))
    logf_ref[...] = logf
    full = jnp.concatenate([past_ref[0], logf], axis=0) if past_len else logf
    lane = lax.broadcasted_iota(jnp.int32, (1, LANES), 1)
    cum = jnp.where(lane < FOX_HEADS, _cumsum_rows(full) * LOG2E, 0.0)
    hi, mid, lo = (p.astype(F32) for p in _split3(cum))
    packed = hi + pltpu.roll(mid, FOX_HEADS, 1) + pltpu.roll(lo, 2 * FOX_HEADS, 1)
    c3_ref[...] = packed.astype(BF16)


def _fox_prep(x16, wf16, bf_pad, past_logf, batch):
    m = x16.shape[0]
    t = m // batch
    past_len = 0 if past_logf is None else past_logf.shape[1]
    sk = past_len + t
    in_specs = [pl.BlockSpec((t, D_MODEL), lambda b: (b, 0)),
                pl.BlockSpec((LANES, D_MODEL), lambda b: (0, 0)),
                pl.BlockSpec((1, LANES), lambda b: (0, 0))]
    args = [x16, wf16, bf_pad]
    if past_len:
        in_specs.append(pl.BlockSpec((1, past_len, LANES), lambda b: (b, 0, 0)))
        args.append(past_logf)
    return pl.pallas_call(
        functools.partial(_fox_prep_kernel, past_len=past_len),
        grid=(batch,),
        in_specs=in_specs,
        out_specs=[pl.BlockSpec((t, LANES), lambda b: (b, 0)),
                   pl.BlockSpec((sk, LANES), lambda b: (b, 0))],
        out_shape=[jax.ShapeDtypeStruct((m, LANES), F32),
                   jax.ShapeDtypeStruct((batch * sk, LANES), BF16)],
        compiler_params=_params("parallel"),
        name="fox_prep",
    )(*args)


def _bias_selectors():
    sel = np.zeros((FOX_HEADS, LANES, 2 * LANES), np.float32)
    for h in range(FOX_HEADS):
        for piece in range(3):
            sel[h, piece * FOX_HEADS + h, piece] = 1.0
            sel[h, piece * FOX_HEADS + h, LANES + 3 + piece] = -1.0
    ones = np.zeros((2, 1, LANES), np.float32)
    ones[0, 0, 3:6] = 1.0
    ones[1, 0, 0:3] = 1.0
    return jnp.asarray(sel, BF16), jnp.asarray(ones, F32)


def _fox_attn_t_kernel(q_ref, c3_ref, sel_ref, one_ref, k_ref, vt_ref, g_ref, o_ref, qq, kk, *, tq):
    t = q_ref.shape[0]
    aug = jnp.dot(c3_ref[...], sel_ref[0], preferred_element_type=F32)
    qq[:, :LANES] = q_ref[...]
    qq[:, LANES:] = (aug[:, :LANES] + one_ref[0]).astype(BF16)
    kk[:, :LANES] = k_ref[0, 0].astype(BF16)
    kk[:, LANES:] = (aug[:, LANES:] + one_ref[1]).astype(BF16)
    key = lax.broadcasted_iota(jnp.int32, (tq, tq), 0)
    qry = lax.broadcasted_iota(jnp.int32, (tq, tq), 1)
    causal = key <= qry
    def scores(r0):
        qi = qq[r0:r0 + tq, :]
        s_d = lax.dot_general(kk[r0:r0 + tq, :], qi, _NT, preferred_element_type=F32)
        s_p = lax.dot_general(kk[:r0, :], qi, _NT, preferred_element_type=F32) if r0 else None
        return jnp.where(causal, s_d, NEG_INF), s_p

    ahead = scores(0)
    for i in range(t // tq):
        r0 = i * tq
        s_d, s_p = ahead
        if r0 + tq < t:
            ahead = scores(r0 + tq)
        m = jnp.max(s_d, axis=0, keepdims=True)
        if r0:
            m = jnp.maximum(m, jnp.max(s_p, axis=0, keepdims=True))
            p_p = jnp.exp2(s_p - m)
        p_d = jnp.exp2(s_d - m)
        l = jnp.sum(p_d, axis=0, keepdims=True)
        o = jnp.dot(vt_ref[0, 0, :, r0:r0 + tq], p_d.astype(BF16), preferred_element_type=F32)
        if r0:
            l = l + jnp.sum(p_p, axis=0, keepdims=True)
            o = o + jnp.dot(vt_ref[0, 0, :, :r0], p_p.astype(BF16), preferred_element_type=F32)
        g = g_ref[r0:r0 + tq, :]
        o_ref[r0:r0 + tq, :] = ((o / l).T * _silu(g)).astype(o_ref.dtype)


def _fox_attn_t(q16, c3, sel, ones, k32, vt16, g_fox, tq):
    batch, _, t, _ = k32.shape
    tok = lambda b, h: (b, h)
    per_head = lambda b, h: (b, h, 0, 0)
    return pl.pallas_call(
        functools.partial(_fox_attn_t_kernel, tq=tq),
        grid=(batch, FOX_HEADS),
        in_specs=[pl.BlockSpec((t, LANES), tok),
                  pl.BlockSpec((t, LANES), lambda b, h: (b, 0)),
                  pl.BlockSpec((1, LANES, 2 * LANES), lambda b, h: (h, 0, 0)),
                  pl.BlockSpec((2, 1, LANES), lambda b, h: (0, 0, 0)),
                  pl.BlockSpec((1, 1, t, LANES), per_head),
                  pl.BlockSpec((1, 1, LANES, t), per_head),
                  pl.BlockSpec((t, LANES), tok)],
        out_specs=pl.BlockSpec((t, LANES), tok),
        out_shape=jax.ShapeDtypeStruct((batch * t, FOX_WIDTH), BF16),
        scratch_shapes=[pltpu.VMEM((t, 2 * LANES), BF16),
                        pltpu.VMEM((t, 2 * LANES), BF16)],
        compiler_params=_params("parallel", "parallel"),
        name="fox_attn_t",
    )(q16, c3, sel, ones, k32, vt16, g_fox)


def _fox_attn_hist_kernel(q_ref, c3_ref, sel_ref, one_ref, k_ref, v_ref, g_ref, kp_ref, vp_ref, o_ref,
                          qq, kk, vv):
    t = q_ref.shape[0]
    past_len = kp_ref.shape[2]
    c3 = c3_ref[...]
    for h in range(FOX_HEADS):
        cols = slice(h * LANES, (h + 1) * LANES)
        aug = jnp.dot(c3, sel_ref[h], preferred_element_type=F32)
        qq[h, :, :LANES] = q_ref[:, cols]
        qq[h, :, LANES:] = (aug[past_len:, :LANES] + one_ref[0]).astype(BF16)
        kk[h, :past_len, :LANES] = kp_ref[0, h].astype(BF16)
        kk[h, past_len:, :LANES] = k_ref[0, h].astype(BF16)
        kk[h, :, LANES:] = (aug[:, LANES:] + one_ref[1]).astype(BF16)
        vv[h, :past_len, :] = vp_ref[0, h].astype(BF16)
        vv[h, past_len:, :] = v_ref[0, h].astype(BF16)
    row = lax.broadcasted_iota(jnp.int32, (t, t), 0)
    col = lax.broadcasted_iota(jnp.int32, (t, t), 1)
    causal = col <= row
    scores = []
    for h in range(FOX_HEADS):
        s_p = lax.dot_general(qq[h], kk[h, :past_len, :], _NT, preferred_element_type=F32)
        s_d = lax.dot_general(qq[h], kk[h, past_len:, :], _NT, preferred_element_type=F32)
        scores.append((s_p, jnp.where(causal, s_d, NEG_INF)))
    for h in range(FOX_HEADS):
        cols = slice(h * LANES, (h + 1) * LANES)
        s_p, s_d = scores[h]
        m = jnp.maximum(jnp.max(s_p, axis=-1, keepdims=True), jnp.max(s_d, axis=-1, keepdims=True))
        p_p = jnp.exp2(s_p - m)
        p_d = jnp.exp2(s_d - m)
        l = jnp.sum(p_d, axis=-1, keepdims=True) + jnp.sum(p_p, axis=-1, keepdims=True)
        o = (jnp.dot(p_d.astype(BF16), vv[h, past_len:, :], preferred_element_type=F32)
             + jnp.dot(p_p.astype(BF16), vv[h, :past_len, :], preferred_element_type=F32))
        o_ref[:, cols] = (o / l * _silu(g_ref[:, cols])).astype(o_ref.dtype)


def _fox_attn_hist(q16, c3, sel, ones, k32, v32, g_fox, k_past, v_past):
    batch, _, t, _ = k32.shape
    past_len = k_past.shape[2]
    sk = past_len + t
    tok = lambda b: (b, 0)
    heads = lambda b: (b, 0, 0, 0)
    return pl.pallas_call(
        _fox_attn_hist_kernel,
        grid=(batch,),
        in_specs=[pl.BlockSpec((t, FOX_WIDTH), tok),
                  pl.BlockSpec((sk, LANES), tok),
                  pl.BlockSpec((FOX_HEADS, LANES, 2 * LANES), lambda b: (0, 0, 0)),
                  pl.BlockSpec((2, 1, LANES), lambda b: (0, 0, 0)),
                  pl.BlockSpec((1, FOX_HEADS, t, LANES), heads),
                  pl.BlockSpec((1, FOX_HEADS, t, LANES), heads),
                  pl.BlockSpec((t, FOX_WIDTH), tok),
                  pl.BlockSpec((1, FOX_HEADS, past_len, LANES), heads),
                  pl.BlockSpec((1, FOX_HEADS, past_len, LANES), heads)],
        out_specs=pl.BlockSpec((t, FOX_WIDTH), tok),
        out_shape=jax.ShapeDtypeStruct((batch * t, FOX_WIDTH), BF16),
        scratch_shapes=[pltpu.VMEM((FOX_HEADS, t, 2 * LANES), BF16),
                        pltpu.VMEM((FOX_HEADS, sk, 2 * LANES), BF16),
                        pltpu.VMEM((FOX_HEADS, sk, LANES), BF16)],
        compiler_params=_params("parallel"),
        name="fox_attn_hist",
    )(q16, c3, sel, ones, k32, v32, g_fox, k_past, v_past)


def _conv_kernel(x_ref, wb_ref, wc_ref, wh_ref, wg_ref, prev_ref, cw_ref, cb_ref, a_ref, tail_ref, *,
                 nb, group):
    rows_total = x_ref.shape[0]
    t = rows_total // nb
    tc = a_ref.shape[1]
    piece = min(group, t)
    w0, w1, w2 = cw_ref[0:1, :], cw_ref[1:2, :], cw_ref[2:3, :]
    bias = cb_ref[...]
    rowi = lax.broadcasted_iota(jnp.int32, (piece, tc), 0)

    def project(gi):
        x = x_ref[gi * group:(gi + 1) * group, :]
        zb = lax.dot_general(x, wb_ref[...], _NT, preferred_element_type=F32)
        zu = (lax.dot_general(x, wc_ref[...], _NT, preferred_element_type=F32)
              * lax.dot_general(x, wh_ref[...], _NT, preferred_element_type=F32))
        zg = lax.dot_general(x, wg_ref[...], _NT, preferred_element_type=F32)
        return zb, zu, zg

    def finish(gi, z, carry):
        zb, zu, zg = z
        for pi in range(group // piece):
            r0 = gi * group + pi * piece
            b, off = divmod(r0, t)
            rows = slice(pi * piece, (pi + 1) * piece)
            u = zu[rows]
            p0, p1 = (prev_ref[b, 0:1, :], prev_ref[b, 1:2, :]) if off == 0 else carry
            u1 = jnp.where(rowi == 0, p1, pltpu.roll(u, 1, 0))
            u2 = jnp.where(rowi == 0, p0, jnp.where(rowi == 1, p1, pltpu.roll(u, 2, 0)))
            c = w0 * u2 + w1 * u1 + w2 * u + bias
            a_ref[r0:r0 + piece, :] = (zb[rows] * c * _silu(zg[rows])).astype(a_ref.dtype)
            carry = (u[piece - 2:piece - 1, :], u[piece - 1:piece, :])
            if off + piece == t:
                tail_ref[b] = u[piece - (CONV_K - 1):, :]
        return carry

    n_groups = rows_total // group
    ahead = project(0)
    carry = None
    for gi in range(n_groups):
        z = ahead
        if gi + 1 < n_groups:
            ahead = project(gi + 1)
        carry = finish(gi, z, carry)


def _conv_branch(x16, wb16, prev, conv_w, conv_b, batch, nb, tc):
    m = x16.shape[0]
    t = m // batch
    wspec = lambda col0: pl.BlockSpec((tc, D_MODEL), lambda c, b: (col0 // tc + c, 0))
    return pl.pallas_call(
        functools.partial(_conv_kernel, nb=nb, group=min(nb * t, CONV_ROW_GROUP)),
        grid=(CONV_WIDTH // tc, batch // nb),
        in_specs=[pl.BlockSpec((nb * t, D_MODEL), lambda c, b: (b, 0)),
                  wspec(WB_CB), wspec(WB_CC), wspec(WB_CH), wspec(WB_CG),
                  pl.BlockSpec((nb, CONV_K - 1, tc), lambda c, b: (b, 0, c)),
                  pl.BlockSpec((CONV_K, tc), lambda c, b: (0, c)),
                  pl.BlockSpec((1, tc), lambda c, b: (0, c))],
        out_specs=[pl.BlockSpec((nb * t, tc), lambda c, b: (b, c)),
                   pl.BlockSpec((nb, CONV_K - 1, tc), lambda c, b: (b, 0, c))],
        out_shape=[jax.ShapeDtypeStruct((m, CONV_WIDTH), BF16),
                   jax.ShapeDtypeStruct((batch, CONV_K - 1, CONV_WIDTH), F32)],
        compiler_params=_params("parallel", "parallel"),
        name="conv_branch",
    )(x16, wb16, wb16, wb16, wb16, prev, conv_w, conv_b)


def _mem_kv_kernel(x_ref, w_ref, o32_ref, o16_ref, *, nb):
    acc = jnp.dot(x_ref[...], w_ref[...], preferred_element_type=F32)
    o16_ref[...] = acc.astype(BF16)
    for b in range(nb):
        for h in range(MEM_HEADS):
            o32_ref[0, b, h] = acc[b * MEM_TOKENS:(b + 1) * MEM_TOKENS,
                                   h * MEM_HEAD_DIM:(h + 1) * MEM_HEAD_DIM]


def _mem_kv(mem16, w16, batch, nb):
    m, k = mem16.shape
    tm = nb * MEM_TOKENS
    return pl.pallas_call(
        functools.partial(_mem_kv_kernel, nb=nb),
        grid=(m // tm, 2),
        in_specs=[pl.BlockSpec((tm, k), lambda i, j: (i, 0)),
                  pl.BlockSpec((k, MEM_WIDTH), lambda i, j: (0, j))],
        out_specs=[pl.BlockSpec((1, nb, MEM_HEADS, MEM_TOKENS, MEM_HEAD_DIM), lambda i, j: (j, i, 0, 0, 0)),
                   pl.BlockSpec((tm, MEM_WIDTH), lambda i, j: (i, j))],
        out_shape=[jax.ShapeDtypeStruct((2, batch, MEM_HEADS, MEM_TOKENS, MEM_HEAD_DIM), F32),
                   jax.ShapeDtypeStruct((m, 2 * MEM_WIDTH), BF16)],
        compiler_params=_params("parallel", "parallel"),
        name="mem_kv",
    )(mem16, w16)


def _mem_attn_kernel(q_ref, k_ref, v_ref, g_ref, o_ref):
    def scores(h):
        cs = slice(h * MEM_HEAD_DIM, (h + 1) * MEM_HEAD_DIM)
        return lax.dot_general(q_ref[:, cs], k_ref[:, cs], _NT, preferred_element_type=F32)

    ahead = scores(0)
    for h in range(MEM_HEADS):
        cs = slice(h * MEM_HEAD_DIM, (h + 1) * MEM_HEAD_DIM)
        s = ahead
        if h + 1 < MEM_HEADS:
            ahead = scores(h + 1)
        m = jnp.max(s, axis=-1, keepdims=True)
        p = jnp.exp(s - m)
        l = jnp.sum(p, axis=-1, keepdims=True)
        o = jnp.dot(p.astype(BF16), v_ref[:, cs], preferred_element_type=F32)
        o_ref[:, cs] = (o / l * _silu(g_ref[:, cs])).astype(o_ref.dtype)


def _mem_attn(mq16, mkv16, g_mem, batch, tm):
    m = mq16.shape[0]
    nt = m // batch // tm
    tok = lambda b, i: (b * nt + i, 0)
    return pl.pallas_call(
        _mem_attn_kernel,
        grid=(batch, nt),
        in_specs=[pl.BlockSpec((tm, MEM_WIDTH), tok),
                  pl.BlockSpec((MEM_TOKENS, MEM_WIDTH), lambda b, i: (b, 0)),
                  pl.BlockSpec((MEM_TOKENS, MEM_WIDTH), lambda b, i: (b, 1)),
                  pl.BlockSpec((tm, MEM_WIDTH), tok)],
        out_specs=pl.BlockSpec((tm, MEM_WIDTH), tok),
        out_shape=jax.ShapeDtypeStruct((m, MEM_WIDTH), BF16),
        compiler_params=_params("parallel", "parallel"),
        name="mem_attn",
    )(mq16, mkv16, mkv16, g_mem)


def _merge_kernel(af_ref, ac_ref, am_ref, x_ref, wf_ref, wc_ref, wm_ref,
                  g1w_ref, g2w_ref, g3w_ref, g1b_ref, g2b_ref, g3b_ref, o_ref):
    x = x_ref[...]
    z1 = jnp.dot(x, g1w_ref[...], preferred_element_type=F32)
    z2 = jnp.dot(x, g2w_ref[...], preferred_element_type=F32)
    z3 = jnp.dot(x, g3w_ref[...], preferred_element_type=F32)
    acc = _sigmoid(z1 + g1b_ref[...]) * jnp.dot(af_ref[...], wf_ref[...], preferred_element_type=F32)
    acc = acc + _sigmoid(z2 + g2b_ref[...]) * jnp.dot(ac_ref[...], wc_ref[...], preferred_element_type=F32)
    acc = acc + _sigmoid(z3 + g3b_ref[...]) * jnp.dot(am_ref[...], wm_ref[...], preferred_element_type=F32)
    o_ref[...] = acc.astype(o_ref.dtype)


def _merge(a_fox, a_conv, a_mem, x16, wfo, wco, wmo, wmerge, bmerge, tm, tn):
    m = x16.shape[0]
    nj = D_MODEL // tn
    row = lambda j, i: (i, 0)
    colb = lambda k: (lambda j, i: (0, j + k * nj))
    return pl.pallas_call(
        _merge_kernel,
        grid=(nj, m // tm),
        in_specs=[pl.BlockSpec((tm, FOX_WIDTH), row),
                  pl.BlockSpec((tm, CONV_WIDTH), row),
                  pl.BlockSpec((tm, MEM_WIDTH), row),
                  pl.BlockSpec((tm, D_MODEL), row),
                  pl.BlockSpec((FOX_WIDTH, tn), colb(0)),
                  pl.BlockSpec((CONV_WIDTH, tn), colb(0)),
                  pl.BlockSpec((MEM_WIDTH, tn), colb(0)),
                  pl.BlockSpec((D_MODEL, tn), colb(0)),
                  pl.BlockSpec((D_MODEL, tn), colb(1)),
                  pl.BlockSpec((D_MODEL, tn), colb(2)),
                  pl.BlockSpec((1, tn), colb(0)),
                  pl.BlockSpec((1, tn), colb(1)),
                  pl.BlockSpec((1, tn), colb(2))],
        out_specs=pl.BlockSpec((tm, tn), lambda j, i: (i, j)),
        out_shape=jax.ShapeDtypeStruct((m, D_MODEL), BF16),
        compiler_params=_params("parallel", "parallel"),
        name="merge",
    )(a_fox, a_conv, a_mem, x16, wfo, wco, wmo, wmerge, wmerge, wmerge, bmerge, bmerge, bmerge)


def _out_kernel(m_ref, w_ref, x_ref, g_ref, b_ref, o_ref, *, alpha):
    h = jnp.dot(m_ref[...], w_ref[...], preferred_element_type=F32)
    r = alpha * x_ref[...] + h
    mu = jnp.mean(r, axis=-1, keepdims=True)
    d = r - mu
    var = jnp.mean(d * d, axis=-1, keepdims=True)
    o_ref[...] = d * lax.rsqrt(var + LN_EPS) * g_ref[...] + b_ref[...]


def _out_proj_norm(m16, wo16, x32, ln_g, ln_b, alpha, tm):
    m = x32.shape[0]
    return pl.pallas_call(
        functools.partial(_out_kernel, alpha=alpha),
        grid=(m // tm,),
        in_specs=[pl.BlockSpec((tm, D_MODEL), lambda i: (i, 0)),
                  pl.BlockSpec((D_MODEL, D_MODEL), lambda i: (0, 0)),
                  pl.BlockSpec((tm, D_MODEL), lambda i: (i, 0)),
                  pl.BlockSpec((1, D_MODEL), lambda i: (0, 0)),
                  pl.BlockSpec((1, D_MODEL), lambda i: (0, 0))],
        out_specs=pl.BlockSpec((tm, D_MODEL), lambda i: (i, 0)),
        out_shape=jax.ShapeDtypeStruct((m, D_MODEL), F32),
        compiler_params=_params("parallel"),
        name="out_proj_norm",
    )(m16, wo16, x32, ln_g, ln_b)


def _layer(x32, batch, wts, alpha, mkv16, prev_conv, k_past, v_past, past_logf, tiles):
    tm = tiles["tm"]
    wa, wb = wts["w_a"], wts["w_b"]
    k32, x16 = _project_heads(x32, wa, FOX_WIDTH, batch, tiles["kv_tm"], "x16", "proj_k")
    if k_past is None:
        v32, vt16 = _project_heads(x16, wa, 2 * FOX_WIDTH, batch, tiles["kv_tm"], "transposed", "proj_v")
    else:
        v32 = _project_heads(x16, wa, 2 * FOX_WIDTH, batch, tiles["kv_tm"], None, "proj_v")
    q16 = _project(x16, wa, 0, FOX_WIDTH, FOX_HEAD_DIM ** -0.5 * LOG2E, BF16, tm, FOX_WIDTH, "proj_q")
    g_fox = _project(x16, wb, WB_GFOX, FOX_WIDTH, 1.0, F32, tm, FOX_WIDTH, "proj_gfox")
    mq16 = _project(x16, wb, WB_MQ, MEM_WIDTH, MEM_HEAD_DIM ** -0.5, BF16, tm, 512, "proj_mq")
    g_mem = _project(x16, wb, WB_GMEM, MEM_WIDTH, 1.0, F32, tm, 512, "proj_gmem")
    logf, c3 = _fox_prep(x16, wts["w_f"], wts["b_f"], past_logf, batch)
    if k_past is None:
        a_fox = _fox_attn_t(q16, c3, wts["sel"], wts["sel_ones"], k32, vt16, g_fox, tiles["tq"])
    else:
        a_fox = _fox_attn_hist(q16, c3, wts["sel"], wts["sel_ones"], k32, v32, g_fox, k_past, v_past)
    a_conv, tail = _conv_branch(x16, wb, prev_conv, wts["conv_w"], wts["conv_b"],
                                batch, tiles["conv_nb"], tiles["conv_tc"])
    a_mem = _mem_attn(mq16, mkv16, g_mem, batch, tiles["mem_tm"])
    m16 = _merge(a_fox, a_conv, a_mem, x16, wts["w_fox_out"], wts["w_conv_out"], wts["w_mem_out"],
                 wts["w_merge"], wts["b_merge"], tiles["merge_tm"], 512)
    y = _out_proj_norm(m16, wts["w_o"], x32, wts["ln_g"], wts["ln_b"], alpha, tiles["out_tm"])
    return y, k32, v32, logf, tail


def _layer_weights(l, w_in, fox_bf, conv_w, conv_b, w_mem_kv, w_fox_out, w_conv_out, w_mem_out,
                   w_merge, b_merge, w_o, ln_g, ln_b):
    w = w_in[l]
    sel, sel_ones = _bias_selectors()
    return {
        "w_a": w[:, :OFF_F].T.astype(BF16),
        "w_b": w[:, OFF_B:].T.astype(BF16),
        "w_f": jnp.pad(w[:, OFF_F:OFF_B].T, ((0, LANES - FOX_HEADS), (0, 0))).astype(BF16),
        "b_f": jnp.pad(fox_bf[l].astype(F32), (0, LANES - FOX_HEADS)).reshape(1, LANES),
        "sel": sel,
        "sel_ones": sel_ones,
        "conv_w": conv_w[l],
        "conv_b": conv_b[l].reshape(1, CONV_WIDTH),
        "w_mem_kv": w_mem_kv[l].astype(BF16),
        "w_fox_out": w_fox_out[l].astype(BF16),
        "w_conv_out": w_conv_out[l].astype(BF16),
        "w_mem_out": w_mem_out[l].astype(BF16),
        "w_merge": w_merge[l].astype(BF16),
        "b_merge": b_merge[l].reshape(1, N_BRANCH * D_MODEL),
        "w_o": w_o[l].astype(BF16),
        "ln_g": ln_g[l].reshape(1, D_MODEL),
        "ln_b": ln_b[l].reshape(1, D_MODEL),
    }


def _to_time_major(x):
    return jnp.transpose(x, (0, 2, 1, 3))


def kernel(x_prompt, x_sample, mem_prompt, cache_fox_k, cache_fox_v, cache_fox_logf, state_conv,
           cache_mem_k, cache_mem_v, w_in, fox_bf, conv_w, conv_b, w_mem_kv, w_fox_out, w_conv_out,
           w_mem_out, w_merge, b_merge, w_o, ln_g, ln_b):
    depth = w_in.shape[0]
    alpha = (2 * depth) ** 0.25
    bp, sp, _ = x_prompt.shape
    bs, ts, _ = x_sample.shape
    prompt_tiles = dict(tm=1024, kv_tm=512, tq=256, conv_nb=1, conv_tc=256, mem_tm=1024, merge_tm=512,
                        out_tm=512)
    sample_tiles = dict(tm=bs * ts, kv_tm=bs * ts, tq=ts, conv_nb=bs, conv_tc=256, mem_tm=ts,
                        merge_tm=bs * ts, out_tm=bs * ts)

    hp = x_prompt.reshape(bp * sp, D_MODEL)
    hs = x_sample.reshape(bs * ts, D_MODEL)
    mem16 = mem_prompt.reshape(bp * MEM_TOKENS, D_MODEL).astype(BF16)
    outs = [[] for _ in range(10)]
    for l in range(depth):
        wts = _layer_weights(l, w_in, fox_bf, conv_w, conv_b, w_mem_kv, w_fox_out, w_conv_out,
                             w_mem_out, w_merge, b_merge, w_o, ln_g, ln_b)
        mkv32, mkv16 = _mem_kv(mem16, wts["w_mem_kv"], bp, 4)
        prev0 = jnp.zeros((bp, CONV_K - 1, CONV_WIDTH), F32)
        hp, k_p, v_p, lf_p, tail_p = _layer(hp, bp, wts, alpha, mkv16, prev0, None, None, None,
                                            prompt_tiles)
        ck = jnp.transpose(cache_fox_k[l], (0, 2, 1, 3))
        cv = jnp.transpose(cache_fox_v[l], (0, 2, 1, 3))
        clf = jnp.pad(cache_fox_logf[l].astype(F32), ((0, 0), (0, 0), (0, LANES - FOX_HEADS)))
        cmkv16 = jnp.concatenate([cache_mem_k[l].reshape(bs * MEM_TOKENS, MEM_WIDTH),
                                  cache_mem_v[l].reshape(bs * MEM_TOKENS, MEM_WIDTH)], axis=1).astype(BF16)
        hs, k_s, v_s, lf_s, tail_s = _layer(hs, bs, wts, alpha, cmkv16, state_conv[l].astype(F32),
                                            ck, cv, clf, sample_tiles)
        vals = (_to_time_major(k_p), _to_time_major(v_p),
                lf_p[:, :FOX_HEADS].reshape(bp, sp, FOX_HEADS), tail_p,
                _to_time_major(mkv32[0]), _to_time_major(mkv32[1]),
                _to_time_major(k_s), _to_time_major(v_s),
                lf_s[:, :FOX_HEADS].reshape(bs, ts, FOX_HEADS), tail_s)
        for acc, val in zip(outs, vals):
            acc.append(val)
    return (hp.reshape(bp, sp, D_MODEL), hs.reshape(bs, ts, D_MODEL)) + tuple(jnp.stack(o) for o in outs)
```

```python
import functools
import math

import numpy as np
import jax
import jax.numpy as jnp
from jax import lax
from jax.experimental import pallas as pl
from jax.experimental.pallas import tpu as pltpu

D_MODEL = 2048
FOX_HEADS = 12
FOX_HEAD_DIM = 128
FOX_WIDTH = FOX_HEADS * FOX_HEAD_DIM
CONV_WIDTH = 1536
CONV_K = 3
MEM_TOKENS = 256
MEM_HEADS = 4
MEM_HEAD_DIM = 256
MEM_WIDTH = MEM_HEADS * MEM_HEAD_DIM
N_BRANCH = 3
LN_EPS = 1e-5
NEG_INF = -1e30
LOG2E = math.log2(math.e)

LANES = 128
VMEM_LIMIT = 56 * 1024 * 1024
CUMSUM_BLOCK = 256
CONV_ROW_GROUP = 512
SCORES_AHEAD = 2
F32 = jnp.float32
BF16 = jnp.bfloat16

IN_SIZES = (FOX_WIDTH, FOX_WIDTH, FOX_WIDTH, FOX_HEADS, FOX_WIDTH,
            CONV_WIDTH, CONV_WIDTH, CONV_WIDTH, CONV_WIDTH, MEM_WIDTH, MEM_WIDTH)
IN_OFFS = tuple(sum(IN_SIZES[:i]) for i in range(len(IN_SIZES) + 1))
WA_GFOX = 3 * FOX_WIDTH
WB_MQ, WB_GMEM = 0, MEM_WIDTH
WB_CB, WB_CC, WB_CH, WB_CG = (2 * MEM_WIDTH + i * CONV_WIDTH for i in range(4))

_NT = (((1,), (1,)), ((), ()))


def _params(*sem):
    return pltpu.CompilerParams(dimension_semantics=sem, vmem_limit_bytes=VMEM_LIMIT)


def _silu(g):
    return g / (1.0 + jnp.exp(-g))


def _sigmoid(g):
    return 1.0 / (1.0 + jnp.exp(-g))


def _proj_kernel(x_ref, w_ref, o_ref, *, scale):
    acc = lax.dot_general(x_ref[...], w_ref[...], _NT, preferred_element_type=F32)
    if scale != 1.0:
        acc = acc * scale
    o_ref[...] = acc.astype(o_ref.dtype)


def _project(x16, w16, col0, n, scale, out_dtype, tm, tn, name):
    m, k = x16.shape
    assert col0 % tn == 0 and n % tn == 0
    j0 = col0 // tn
    return pl.pallas_call(
        functools.partial(_proj_kernel, scale=scale),
        grid=(m // tm, n // tn),
        in_specs=[pl.BlockSpec((tm, k), lambda i, j: (i, 0)),
                  pl.BlockSpec((tn, k), lambda i, j: (j0 + j, 0))],
        out_specs=pl.BlockSpec((tm, tn), lambda i, j: (i, j)),
        out_shape=jax.ShapeDtypeStruct((m, n), out_dtype),
        compiler_params=_params("parallel", "parallel"),
        name=name,
    )(x16, w16)


def _proj_heads_kernel(*refs, nb, extra):
    if extra == "first":
        x_ref, w_ref, wf_ref, o_ref, x16_ref, zf_ref = refs
        x16 = x_ref[...].astype(BF16)
        x16_ref[...] = x16
        zf_ref[...] = lax.dot_general(x16, wf_ref[...], _NT, preferred_element_type=F32)
    else:
        x_ref, w_ref, o_ref, *extra_refs = refs
        x16 = x_ref[...]
    acc = lax.dot_general(x16, w_ref[...], _NT, preferred_element_type=F32)
    t = x_ref.shape[0] // nb
    for b in range(nb):
        for h in range(FOX_HEADS):
            tile = acc[b * t:(b + 1) * t, h * LANES:(h + 1) * LANES]
            o_ref[b, h] = tile
            if extra == "transposed":
                extra_refs[0][b, h] = tile.T.astype(BF16)


def _project_heads(x, w16, col0, batch, tm, extra, name, wf16=None):
    m, k = x.shape
    t = m // batch
    j0 = col0 // FOX_WIDTH
    if t >= tm:
        nb, tb, per = 1, tm, t // tm
        o_map = lambda i: (i // per, 0, i % per, 0)
        t_map = lambda i: (i // per, 0, 0, i % per)
    else:
        nb, tb = tm // t, t
        o_map = t_map = lambda i: (i, 0, 0, 0)
    out_specs = [pl.BlockSpec((nb, FOX_HEADS, tb, LANES), o_map)]
    out_shape = [jax.ShapeDtypeStruct((batch, FOX_HEADS, t, LANES), F32)]
    in_specs = [pl.BlockSpec((tm, k), lambda i: (i, 0)),
                pl.BlockSpec((FOX_WIDTH, k), lambda i: (j0, 0))]
    args = [x, w16]
    if extra == "first":
        in_specs.append(pl.BlockSpec((LANES, k), lambda i: (0, 0)))
        args.append(wf16)
        out_specs += [pl.BlockSpec((tm, k), lambda i: (i, 0)), pl.BlockSpec((tm, LANES), lambda i: (i, 0))]
        out_shape += [jax.ShapeDtypeStruct((m, k), BF16), jax.ShapeDtypeStruct((m, LANES), F32)]
    elif extra == "transposed":
        out_specs.append(pl.BlockSpec((nb, FOX_HEADS, LANES, tb), t_map))
        out_shape.append(jax.ShapeDtypeStruct((batch, FOX_HEADS, LANES, t), BF16))
    res = pl.pallas_call(
        functools.partial(_proj_heads_kernel, nb=nb, extra=extra),
        grid=(m // tm,),
        in_specs=in_specs,
        out_specs=out_specs,
        out_shape=out_shape,
        compiler_params=_params("parallel"),
        name=name,
    )(*args)
    return res if extra else res[0]


def _split3(x):
    hi = x.astype(BF16)
    rem = x - hi.astype(F32)
    mid = rem.astype(BF16)
    lo = (rem - mid.astype(F32)).astype(BF16)
    return hi, mid, lo


def _cumsum_rows(x):
    s = x.shape[0]
    r = lax.broadcasted_iota(jnp.int32, (CUMSUM_BLOCK, CUMSUM_BLOCK), 0)
    c = lax.broadcasted_iota(jnp.int32, (CUMSUM_BLOCK, CUMSUM_BLOCK), 1)
    tri = (r >= c).astype(BF16)
    carry = jnp.zeros((1, x.shape[1]), F32)
    out = []
    for start in range(0, s, CUMSUM_BLOCK):
        n = min(CUMSUM_BLOCK, s - start)
        hi, mid, lo = _split3(x[start:start + n])
        t = tri[:n, :n]
        cb = (jnp.dot(t, hi, preferred_element_type=F32)
              + jnp.dot(t, mid, preferred_element_type=F32)
              + jnp.dot(t, lo, preferred_element_type=F32)) + carry
        carry = cb[n - 1:n]
        out.append(cb)
    return jnp.concatenate(out, axis=0) if len(out) > 1 else out[0]


def _fox_prep_kernel(*refs, past_len):
    if past_len:
        zf_ref, bf_ref, past_ref, logf_ref, c3_ref = refs
    else:
        zf_ref, bf_ref, logf_ref, c3_ref = refs
    z = zf_ref[...] + bf_ref[...]
    logf = jnp.minimum(z, 0.0) - jnp.log1p(jnp.exp(-jnp.abs(z)))
    logf_ref[...] = logf
    full = jnp.concatenate([past_ref[0], logf], axis=0) if past_len else logf
    lane = lax.broadcasted_iota(jnp.int32, (1, LANES), 1)
    cum = jnp.where(lane < FOX_HEADS, _cumsum_rows(full) * LOG2E, 0.0)
    hi, mid, lo = (p.astype(F32) for p in _split3(cum))
    packed = hi + pltpu.roll(mid, FOX_HEADS, 1) + pltpu.roll(lo, 2 * FOX_HEADS, 1)
    c3_ref[...] = packed.astype(BF16)


def _fox_prep(zf, bf_pad, past_logf, batch):
    m = zf.shape[0]
    t = m // batch
    past_len = 0 if past_logf is None else past_logf.shape[1]
    sk = past_len + t
    in_specs = [pl.BlockSpec((t, LANES), lambda b: (b, 0)),
                pl.BlockSpec((1, LANES), lambda b: (0, 0))]
    args = [zf, bf_pad]
    if past_len:
        in_specs.append(pl.BlockSpec((1, past_len, LANES), lambda b: (b, 0, 0)))
        args.append(past_logf)
    return pl.pallas_call(
        functools.partial(_fox_prep_kernel, past_len=past_len),
        grid=(batch,),
        in_specs=in_specs,
        out_specs=[pl.BlockSpec((t, LANES), lambda b: (b, 0)),
                   pl.BlockSpec((sk, LANES), lambda b: (b, 0))],
        out_shape=[jax.ShapeDtypeStruct((m, LANES), F32),
                   jax.ShapeDtypeStruct((batch * sk, LANES), BF16)],
        compiler_params=_params("parallel"),
        name="fox_prep",
    )(*args)


def _bias_selectors():
    sel = np.zeros((FOX_HEADS, LANES, 2 * LANES), np.float32)
    for h in range(FOX_HEADS):
        for piece in range(3):
            sel[h, piece * FOX_HEADS + h, piece] = 1.0
            sel[h, piece * FOX_HEADS + h, LANES + 3 + piece] = -1.0
    ones = np.zeros((2, 1, LANES), np.float32)
    ones[0, 0, 3:6] = 1.0
    ones[1, 0, 0:3] = 1.0
    return jnp.asarray(sel, BF16), jnp.asarray(ones, F32)


def _fox_attn_t_kernel(q_ref, c3_ref, sel_ref, one_ref, k_ref, vt_ref, g_ref, o_ref, qq, kk, *, tq):
    t = q_ref.shape[0]
    aug = jnp.dot(c3_ref[...], sel_ref[0], preferred_element_type=F32)
    qq[:, :LANES] = q_ref[...]
    qq[:, LANES:] = (aug[:, :LANES] + one_ref[0]).astype(BF16)
    kk[:, :LANES] = k_ref[0, 0].astype(BF16)
    kk[:, LANES:] = (aug[:, LANES:] + one_ref[1]).astype(BF16)
    key = lax.broadcasted_iota(jnp.int32, (tq, tq), 0)
    qry = lax.broadcasted_iota(jnp.int32, (tq, tq), 1)
    causal = key <= qry
    def scores(r0):
        qi = qq[r0:r0 + tq, :]
        s_d = lax.dot_general(kk[r0:r0 + tq, :], qi, _NT, preferred_element_type=F32)
        s_p = lax.dot_general(kk[:r0, :], qi, _NT, preferred_element_type=F32) if r0 else None
        return jnp.where(causal, s_d, NEG_INF), s_p

    order = [j * tq for j in range(t // tq)]
    ahead = [scores(r) for r in order[:SCORES_AHEAD]]
    for i, r0 in enumerate(order):
        s_d, s_p = ahead.pop(0)
        if i + SCORES_AHEAD < len(order):
            ahead.append(scores(order[i + SCORES_AHEAD]))
        m = jnp.max(s_d, axis=0, keepdims=True)
        if r0:
            m = jnp.maximum(m, jnp.max(s_p, axis=0, keepdims=True))
            p_p = jnp.exp2(s_p - m)
        p_d = jnp.exp2(s_d - m)
        l = jnp.sum(p_d, axis=0, keepdims=True)
        o = jnp.dot(vt_ref[0, 0, :, r0:r0 + tq], p_d.astype(BF16), preferred_element_type=F32)
        if r0:
            l = l + jnp.sum(p_p, axis=0, keepdims=True)
            o = o + jnp.dot(vt_ref[0, 0, :, :r0], p_p.astype(BF16), preferred_element_type=F32)
        g = g_ref[r0:r0 + tq, :]
        o_ref[r0:r0 + tq, :] = ((o / l).T * _silu(g)).astype(o_ref.dtype)


def _fox_attn_t(q16, c3, sel, ones, k32, vt16, g_fox, tq):
    batch, _, t, _ = k32.shape
    tok = lambda b, h: (b, h)
    per_head = lambda b, h: (b, h, 0, 0)
    return pl.pallas_call(
        functools.partial(_fox_attn_t_kernel, tq=tq),
        grid=(batch, FOX_HEADS),
        in_specs=[pl.BlockSpec((t, LANES), tok),
                  pl.BlockSpec((t, LANES), lambda b, h: (b, 0)),
                  pl.BlockSpec((1, LANES, 2 * LANES), lambda b, h: (h, 0, 0)),
                  pl.BlockSpec((2, 1, LANES), lambda b, h: (0, 0, 0)),
                  pl.BlockSpec((1, 1, t, LANES), per_head),
                  pl.BlockSpec((1, 1, LANES, t), per_head),
                  pl.BlockSpec((t, LANES), tok)],
        out_specs=pl.BlockSpec((t, LANES), tok),
        out_shape=jax.ShapeDtypeStruct((batch * t, FOX_WIDTH), BF16),
        scratch_shapes=[pltpu.VMEM((t, 2 * LANES), BF16),
                        pltpu.VMEM((t, 2 * LANES), BF16)],
        compiler_params=_params("parallel", "parallel"),
        name="fox_attn_t",
    )(q16, c3, sel, ones, k32, vt16, g_fox)


def _fox_attn_hist_kernel(q_ref, c3_ref, sel_ref, one_ref, k_ref, v_ref, g_ref, kp_ref, vp_ref, o_ref,
                          qq, kk, vv):
    t = q_ref.shape[0]
    past_len = kp_ref.shape[2]
    c3 = c3_ref[...]
    for h in range(FOX_HEADS):
        cols = slice(h * LANES, (h + 1) * LANES)
        aug = jnp.dot(c3, sel_ref[h], preferred_element_type=F32)
        qq[h, :, :LANES] = q_ref[:, cols]
        qq[h, :, LANES:] = (aug[past_len:, :LANES] + one_ref[0]).astype(BF16)
        kk[h, :past_len, :LANES] = kp_ref[0, h].astype(BF16)
        kk[h, past_len:, :LANES] = k_ref[0, h].astype(BF16)
        kk[h, :, LANES:] = (aug[:, LANES:] + one_ref[1]).astype(BF16)
        vv[h, :past_len, :] = vp_ref[0, h].astype(BF16)
        vv[h, past_len:, :] = v_ref[0, h].astype(BF16)
    row = lax.broadcasted_iota(jnp.int32, (t, t), 0)
    col = lax.broadcasted_iota(jnp.int32, (t, t), 1)
    causal = col <= row
    scores = []
    for h in range(FOX_HEADS):
        s_p = lax.dot_general(qq[h], kk[h, :past_len, :], _NT, preferred_element_type=F32)
        s_d = lax.dot_general(qq[h], kk[h, past_len:, :], _NT, preferred_element_type=F32)
        scores.append((s_p, jnp.where(causal, s_d, NEG_INF)))
    for h in range(FOX_HEADS):
        cols = slice(h * LANES, (h + 1) * LANES)
        s_p, s_d = scores[h]
        m = jnp.maximum(jnp.max(s_p, axis=-1, keepdims=True), jnp.max(s_d, axis=-1, keepdims=True))
        p_p = jnp.exp2(s_p - m)
        p_d = jnp.exp2(s_d - m)
        l = jnp.sum(p_d, axis=-1, keepdims=True) + jnp.sum(p_p, axis=-1, keepdims=True)
        o = (jnp.dot(p_d.astype(BF16), vv[h, past_len:, :], preferred_element_type=F32)
             + jnp.dot(p_p.astype(BF16), vv[h, :past_len, :], preferred_element_type=F32))
        o_ref[:, cols] = (o / l * _silu(g_ref[:, cols])).astype(o_ref.dtype)


def _fox_attn_hist(q16, c3, sel, ones, k32, v32, g_fox, k_past, v_past):
    batch, _, t, _ = k32.shape
    past_len = k_past.shape[2]
    sk = past_len + t
    tok = lambda b: (b, 0)
    heads = lambda b: (b, 0, 0, 0)
    return pl.pallas_call(
        _fox_attn_hist_kernel,
        grid=(batch,),
        in_specs=[pl.BlockSpec((t, FOX_WIDTH), tok),
                  pl.BlockSpec((sk, LANES), tok),
                  pl.BlockSpec((FOX_HEADS, LANES, 2 * LANES), lambda b: (0, 0, 0)),
                  pl.BlockSpec((2, 1, LANES), lambda b: (0, 0, 0)),
                  pl.BlockSpec((1, FOX_HEADS, t, LANES), heads),
                  pl.BlockSpec((1, FOX_HEADS, t, LANES), heads),
                  pl.BlockSpec((t, FOX_WIDTH), tok),
                  pl.BlockSpec((1, FOX_HEADS, past_len, LANES), heads),
                  pl.BlockSpec((1, FOX_HEADS, past_len, LANES), heads)],
        out_specs=pl.BlockSpec((t, FOX_WIDTH), tok),
        out_shape=jax.ShapeDtypeStruct((batch * t, FOX_WIDTH), BF16),
        scratch_shapes=[pltpu.VMEM((FOX_HEADS, t, 2 * LANES), BF16),
                        pltpu.VMEM((FOX_HEADS, sk, 2 * LANES), BF16),
                        pltpu.VMEM((FOX_HEADS, sk, LANES), BF16)],
        compiler_params=_params("parallel"),
        name="fox_attn_hist",
    )(q16, c3, sel, ones, k32, v32, g_fox, k_past, v_past)


def _conv_kernel(x_ref, wb_ref, wc_ref, wh_ref, wg_ref, prev_ref, cw_ref, cb_ref, a_ref, tail_ref, *,
                 nb, group):
    rows_total = x_ref.shape[0]
    t = rows_total // nb
    tc = a_ref.shape[1]
    piece = min(group, t)
    w0, w1, w2 = cw_ref[0:1, :], cw_ref[1:2, :], cw_ref[2:3, :]
    bias = cb_ref[...]
    rowi = lax.broadcasted_iota(jnp.int32, (piece, tc), 0)

    def project(gi):
        x = x_ref[gi * group:(gi + 1) * group, :]
        zb = lax.dot_general(x, wb_ref[...], _NT, preferred_element_type=F32)
        zu = (lax.dot_general(x, wc_ref[...], _NT, preferred_element_type=F32)
              * lax.dot_general(x, wh_ref[...], _NT, preferred_element_type=F32))
        zg = lax.dot_general(x, wg_ref[...], _NT, preferred_element_type=F32)
        return zb, zu, zg

    def finish(gi, z, carry):
        zb, zu, zg = z
        for pi in range(group // piece):
            r0 = gi * group + pi * piece
            b, off = divmod(r0, t)
            rows = slice(pi * piece, (pi + 1) * piece)
            u = zu[rows]
            p0, p1 = (prev_ref[b, 0:1, :], prev_ref[b, 1:2, :]) if off == 0 else carry
            u1 = jnp.where(rowi == 0, p1, pltpu.roll(u, 1, 0))
            u2 = jnp.where(rowi == 0, p0, jnp.where(rowi == 1, p1, pltpu.roll(u, 2, 0)))
            c = w0 * u2 + w1 * u1 + w2 * u + bias
            a_ref[r0:r0 + piece, :] = (zb[rows] * c * _silu(zg[rows])).astype(a_ref.dtype)
            carry = (u[piece - 2:piece - 1, :], u[piece - 1:piece, :])
            if off + piece == t:
                tail_ref[b] = u[piece - (CONV_K - 1):, :]
        return carry

    n_groups = rows_total // group
    ahead = project(0)
    carry = None
    for gi in range(n_groups):
        z = ahead
        if gi + 1 < n_groups:
            ahead = project(gi + 1)
        carry = finish(gi, z, carry)


def _conv_branch(x16, wb16, prev, conv_w, conv_b, batch, nb, tc):
    m = x16.shape[0]
    t = m // batch
    wspec = lambda col0: pl.BlockSpec((tc, D_MODEL), lambda c, b: (col0 // tc + c, 0))
    return pl.pallas_call(
        functools.partial(_conv_kernel, nb=nb, group=min(nb * t, CONV_ROW_GROUP)),
        grid=(CONV_WIDTH // tc, batch // nb),
        in_specs=[pl.BlockSpec((nb * t, D_MODEL), lambda c, b: (b, 0)),
                  wspec(WB_CB), wspec(WB_CC), wspec(WB_CH), wspec(WB_CG),
                  pl.BlockSpec((nb, CONV_K - 1, tc), lambda c, b: (b, 0, c)),
                  pl.BlockSpec((CONV_K, tc), lambda c, b: (0, c)),
                  pl.BlockSpec((1, tc), lambda c, b: (0, c))],
        out_specs=[pl.BlockSpec((nb * t, tc), lambda c, b: (b, c)),
                   pl.BlockSpec((nb, CONV_K - 1, tc), lambda c, b: (b, 0, c))],
        out_shape=[jax.ShapeDtypeStruct((m, CONV_WIDTH), BF16),
                   jax.ShapeDtypeStruct((batch, CONV_K - 1, CONV_WIDTH), F32)],
        compiler_params=_params("parallel", "parallel"),
        name="conv_branch",
    )(x16, wb16, wb16, wb16, wb16, prev, conv_w, conv_b)


def _mem_kv_kernel(x_ref, w_ref, o32_ref, o16_ref, *, nb):
    acc = jnp.dot(x_ref[...], w_ref[...], preferred_element_type=F32)
    o16_ref[...] = acc.astype(BF16)
    for b in range(nb):
        for h in range(MEM_HEADS):
            o32_ref[0, b, h] = acc[b * MEM_TOKENS:(b + 1) * MEM_TOKENS,
                                   h * MEM_HEAD_DIM:(h + 1) * MEM_HEAD_DIM]


def _mem_kv(mem16, w16, batch, nb):
    m, k = mem16.shape
    tm = nb * MEM_TOKENS
    return pl.pallas_call(
        functools.partial(_mem_kv_kernel, nb=nb),
        grid=(m // tm, 2),
        in_specs=[pl.BlockSpec((tm, k), lambda i, j: (i, 0)),
                  pl.BlockSpec((k, MEM_WIDTH), lambda i, j: (0, j))],
        out_specs=[pl.BlockSpec((1, nb, MEM_HEADS, MEM_TOKENS, MEM_HEAD_DIM), lambda i, j: (j, i, 0, 0, 0)),
                   pl.BlockSpec((tm, MEM_WIDTH), lambda i, j: (i, j))],
        out_shape=[jax.ShapeDtypeStruct((2, batch, MEM_HEADS, MEM_TOKENS, MEM_HEAD_DIM), F32),
                   jax.ShapeDtypeStruct((m, 2 * MEM_WIDTH), BF16)],
        compiler_params=_params("parallel", "parallel"),
        name="mem_kv",
    )(mem16, w16)


def _mem_attn_kernel(q_ref, k_ref, v_ref, g_ref, o_ref):
    def scores(h):
        cs = slice(h * MEM_HEAD_DIM, (h + 1) * MEM_HEAD_DIM)
        return lax.dot_general(q_ref[:, cs], k_ref[:, cs], _NT, preferred_element_type=F32)

    ahead = scores(0)
    for h in range(MEM_HEADS):
        cs = slice(h * MEM_HEAD_DIM, (h + 1) * MEM_HEAD_DIM)
        s = ahead
        if h + 1 < MEM_HEADS:
            ahead = scores(h + 1)
        m = jnp.max(s, axis=-1, keepdims=True)
        p = jnp.exp(s - m)
        l = jnp.sum(p, axis=-1, keepdims=True)
        o = jnp.dot(p.astype(BF16), v_ref[:, cs], preferred_element_type=F32)
        o_ref[:, cs] = (o / l * _silu(g_ref[:, cs])).astype(o_ref.dtype)


def _mem_attn(mq16, mkv16, g_mem, batch, tm):
    m = mq16.shape[0]
    nt = m // batch // tm
    tok = lambda b, i: (b * nt + i, 0)
    return pl.pallas_call(
        _mem_attn_kernel,
        grid=(batch, nt),
        in_specs=[pl.BlockSpec((tm, MEM_WIDTH), tok),
                  pl.BlockSpec((MEM_TOKENS, MEM_WIDTH), lambda b, i: (b, 0)),
                  pl.BlockSpec((MEM_TOKENS, MEM_WIDTH), lambda b, i: (b, 1)),
                  pl.BlockSpec((tm, MEM_WIDTH), tok)],
        out_specs=pl.BlockSpec((tm, MEM_WIDTH), tok),
        out_shape=jax.ShapeDtypeStruct((m, MEM_WIDTH), BF16),
        compiler_params=_params("parallel", "parallel"),
        name="mem_attn",
    )(mq16, mkv16, mkv16, g_mem)


def _merge_kernel(af_ref, ac_ref, am_ref, x_ref, wf_ref, wc_ref, wm_ref,
                  g1w_ref, g2w_ref, g3w_ref, g1b_ref, g2b_ref, g3b_ref, o_ref):
    x = x_ref[...]
    z1 = jnp.dot(x, g1w_ref[...], preferred_element_type=F32)
    z2 = jnp.dot(x, g2w_ref[...], preferred_element_type=F32)
    z3 = jnp.dot(x, g3w_ref[...], preferred_element_type=F32)
    acc = _sigmoid(z1 + g1b_ref[...]) * jnp.dot(af_ref[...], wf_ref[...], preferred_element_type=F32)
    acc = acc + _sigmoid(z2 + g2b_ref[...]) * jnp.dot(ac_ref[...], wc_ref[...], preferred_element_type=F32)
    acc = acc + _sigmoid(z3 + g3b_ref[...]) * jnp.dot(am_ref[...], wm_ref[...], preferred_element_type=F32)
    o_ref[...] = acc.astype(o_ref.dtype)


def _merge(a_fox, a_conv, a_mem, x16, wfo, wco, wmo, wmerge, bmerge, tm, tn):
    m = x16.shape[0]
    nj = D_MODEL // tn
    row = lambda j, i: (i, 0)
    colb = lambda k: (lambda j, i: (0, j + k * nj))
    return pl.pallas_call(
        _merge_kernel,
        grid=(nj, m // tm),
        in_specs=[pl.BlockSpec((tm, FOX_WIDTH), row),
                  pl.BlockSpec((tm, CONV_WIDTH), row),
                  pl.BlockSpec((tm, MEM_WIDTH), row),
                  pl.BlockSpec((tm, D_MODEL), row),
                  pl.BlockSpec((FOX_WIDTH, tn), colb(0)),
                  pl.BlockSpec((CONV_WIDTH, tn), colb(0)),
                  pl.BlockSpec((MEM_WIDTH, tn), colb(0)),
                  pl.BlockSpec((D_MODEL, tn), colb(0)),
                  pl.BlockSpec((D_MODEL, tn), colb(1)),
                  pl.BlockSpec((D_MODEL, tn), colb(2)),
                  pl.BlockSpec((1, tn), colb(0)),
                  pl.BlockSpec((1, tn), colb(1)),
                  pl.BlockSpec((1, tn), colb(2))],
        out_specs=pl.BlockSpec((tm, tn), lambda j, i: (i, j)),
        out_shape=jax.ShapeDtypeStruct((m, D_MODEL), BF16),
        compiler_params=_params("parallel", "parallel"),
        name="merge",
    )(a_fox, a_conv, a_mem, x16, wfo, wco, wmo, wmerge, wmerge, wmerge, bmerge, bmerge, bmerge)


def _out_kernel(m_ref, w_ref, x_ref, g_ref, b_ref, o_ref, *, alpha):
    half = m_ref.shape[0] // 2
    hs = [jnp.dot(m_ref[i * half:(i + 1) * half, :], w_ref[...], preferred_element_type=F32)
          for i in range(2)]
    for i, h in enumerate(hs):
        rows = slice(i * half, (i + 1) * half)
        r = alpha * x_ref[rows, :] + h
        mu = jnp.mean(r, axis=-1, keepdims=True)
        d = r - mu
        var = jnp.mean(d * d, axis=-1, keepdims=True)
        o_ref[rows, :] = d * lax.rsqrt(var + LN_EPS) * g_ref[...] + b_ref[...]


def _out_proj_norm(m16, wo16, x32, ln_g, ln_b, alpha, tm):
    m = x32.shape[0]
    return pl.pallas_call(
        functools.partial(_out_kernel, alpha=alpha),
        grid=(m // tm,),
        in_specs=[pl.BlockSpec((tm, D_MODEL), lambda i: (i, 0)),
                  pl.BlockSpec((D_MODEL, D_MODEL), lambda i: (0, 0)),
                  pl.BlockSpec((tm, D_MODEL), lambda i: (i, 0)),
                  pl.BlockSpec((1, D_MODEL), lambda i: (0, 0)),
                  pl.BlockSpec((1, D_MODEL), lambda i: (0, 0))],
        out_specs=pl.BlockSpec((tm, D_MODEL), lambda i: (i, 0)),
        out_shape=jax.ShapeDtypeStruct((m, D_MODEL), F32),
        compiler_params=_params("parallel"),
        name="out_proj_norm",
    )(m16, wo16, x32, ln_g, ln_b)


def _layer(x32, batch, wts, alpha, mkv16, prev_conv, k_past, v_past, past_logf, tiles):
    tm = tiles["tm"]
    wa, wb = wts["w_a"], wts["w_b"]
    k32, x16, zf = _project_heads(x32, wa, FOX_WIDTH, batch, tiles["kv_tm"], "first", "proj_k", wts["w_f"])
    if k_past is None:
        v32, vt16 = _project_heads(x16, wa, 2 * FOX_WIDTH, batch, tiles["kv_tm"], "transposed", "proj_v")
    else:
        v32 = _project_heads(x16, wa, 2 * FOX_WIDTH, batch, tiles["kv_tm"], None, "proj_v")
    q16 = _project(x16, wa, 0, FOX_WIDTH, FOX_HEAD_DIM ** -0.5 * LOG2E, BF16, tm, FOX_WIDTH, "proj_q")
    g_fox = _project(x16, wa, WA_GFOX, FOX_WIDTH, 1.0, F32, tm, FOX_WIDTH, "proj_gfox")
    mq16 = _project(x16, wb, WB_MQ, MEM_WIDTH, MEM_HEAD_DIM ** -0.5, BF16, tm, MEM_WIDTH, "proj_mq")
    g_mem = _project(x16, wb, WB_GMEM, MEM_WIDTH, 1.0, F32, tm, MEM_WIDTH, "proj_gmem")
    logf, c3 = _fox_prep(zf, wts["b_f"], past_logf, batch)
    if k_past is None:
        a_fox = _fox_attn_t(q16, c3, wts["sel"], wts["sel_ones"], k32, vt16, g_fox, tiles["tq"])
    else:
        a_fox = _fox_attn_hist(q16, c3, wts["sel"], wts["sel_ones"], k32, v32, g_fox, k_past, v_past)
    a_conv, tail = _conv_branch(x16, wb, prev_conv, wts["conv_w"], wts["conv_b"],
                                batch, tiles["conv_nb"], tiles["conv_tc"])
    a_mem = _mem_attn(mq16, mkv16, g_mem, batch, tiles["mem_tm"])
    m16 = _merge(a_fox, a_conv, a_mem, x16, wts["w_fox_out"], wts["w_conv_out"], wts["w_mem_out"],
                 wts["w_merge"], wts["b_merge"], tiles["merge_tm"], 512)
    y = _out_proj_norm(m16, wts["w_o"], x32, wts["ln_g"], wts["ln_b"], alpha, tiles["out_tm"])
    return y, k32, v32, logf, tail


def _layer_weights(l, w_in, fox_bf, conv_w, conv_b, w_mem_kv, w_fox_out, w_conv_out, w_mem_out,
                   w_merge, b_merge, w_o, ln_g, ln_b):
    wt = w_in[l].T.astype(BF16)
    part = lambda i, j: wt[IN_OFFS[i]:IN_OFFS[j]]
    sel, sel_ones = _bias_selectors()
    return {
        "w_a": jnp.concatenate([part(0, 3), part(4, 5)], axis=0),
        "w_b": jnp.concatenate([part(9, 11), part(5, 9)], axis=0),
        "w_f": jnp.pad(part(3, 4), ((0, LANES - FOX_HEADS), (0, 0))),
        "b_f": jnp.pad(fox_bf[l].astype(F32), (0, LANES - FOX_HEADS)).reshape(1, LANES),
        "sel": sel,
        "sel_ones": sel_ones,
        "conv_w": conv_w[l],
        "conv_b": conv_b[l].reshape(1, CONV_WIDTH),
        "w_mem_kv": w_mem_kv[l].astype(BF16),
        "w_fox_out": w_fox_out[l].astype(BF16),
        "w_conv_out": w_conv_out[l].astype(BF16),
        "w_mem_out": w_mem_out[l].astype(BF16),
        "w_merge": w_merge[l].astype(BF16),
        "b_merge": b_merge[l].reshape(1, N_BRANCH * D_MODEL),
        "w_o": w_o[l].astype(BF16),
        "ln_g": ln_g[l].reshape(1, D_MODEL),
        "ln_b": ln_b[l].reshape(1, D_MODEL),
    }


def _to_time_major(x):
    return jnp.transpose(x, (0, 2, 1, 3))


def kernel(x_prompt, x_sample, mem_prompt, cache_fox_k, cache_fox_v, cache_fox_logf, state_conv,
           cache_mem_k, cache_mem_v, w_in, fox_bf, conv_w, conv_b, w_mem_kv, w_fox_out, w_conv_out,
           w_mem_out, w_merge, b_merge, w_o, ln_g, ln_b):
    depth = w_in.shape[0]
    alpha = (2 * depth) ** 0.25
    bp, sp, _ = x_prompt.shape
    bs, ts, _ = x_sample.shape
    prompt_tiles = dict(tm=1024, kv_tm=512, tq=256, conv_nb=1, conv_tc=256, mem_tm=1024, merge_tm=512,
                        out_tm=512)
    sample_tiles = dict(tm=bs * ts, kv_tm=bs * ts, tq=ts, conv_nb=bs, conv_tc=256, mem_tm=ts,
                        merge_tm=bs * ts, out_tm=bs * ts)

    hp = x_prompt.reshape(bp * sp, D_MODEL)
    hs = x_sample.reshape(bs * ts, D_MODEL)
    mem16 = mem_prompt.reshape(bp * MEM_TOKENS, D_MODEL).astype(BF16)
    outs = [[] for _ in range(10)]
    for l in range(depth):
        wts = _layer_weights(l, w_in, fox_bf, conv_w, conv_b, w_mem_kv, w_fox_out, w_conv_out,
                             w_mem_out, w_merge, b_merge, w_o, ln_g, ln_b)
        mkv32, mkv16 = _mem_kv(mem16, wts["w_mem_kv"], bp, 4)
        prev0 = jnp.zeros((bp, CONV_K - 1, CONV_WIDTH), F32)
        hp, k_p, v_p, lf_p, tail_p = _layer(hp, bp, wts, alpha, mkv16, prev0, None, None, None,
                                            prompt_tiles)
        ck = jnp.transpose(cache_fox_k[l], (0, 2, 1, 3))
        cv = jnp.transpose(cache_fox_v[l], (0, 2, 1, 3))
        clf = jnp.pad(cache_fox_logf[l].astype(F32), ((0, 0), (0, 0), (0, LANES - FOX_HEADS)))
        cmkv16 = jnp.concatenate([cache_mem_k[l].reshape(bs * MEM_TOKENS, MEM_WIDTH),
                                  cache_mem_v[l].reshape(bs * MEM_TOKENS, MEM_WIDTH)], axis=1).astype(BF16)
        hs, k_s, v_s, lf_s, tail_s = _layer(hs, bs, wts, alpha, cmkv16, state_conv[l].astype(F32),
                                            ck, cv, clf, sample_tiles)
        vals = (_to_time_major(k_p), _to_time_major(v_p),
                lf_p[:, :FOX_HEADS].reshape(bp, sp, FOX_HEADS), tail_p,
                _to_time_major(mkv32[0]), _to_time_major(mkv32[1]),
                _to_time_major(k_s), _to_time_major(v_s),
                lf_s[:, :FOX_HEADS].reshape(bs, ts, FOX_HEADS), tail_s)
        for acc, val in zip(outs, vals):
            acc.append(val)
    return (hp.reshape(bp, sp, D_MODEL), hs.reshape(bs, ts, D_MODEL)) + tuple(jnp.stack(o) for o in outs)
```

```python
import functools
import math

import numpy as np
import jax
import jax.numpy as jnp
from jax import lax
from jax.experimental import pallas as pl
from jax.experimental.pallas import tpu as pltpu

D_MODEL = 2048
FOX_HEADS = 12
FOX_HEAD_DIM = 128
FOX_WIDTH = FOX_HEADS * FOX_HEAD_DIM
CONV_WIDTH = 1536
CONV_K = 3
MEM_TOKENS = 256
MEM_HEADS = 4
MEM_HEAD_DIM = 256
MEM_WIDTH = MEM_HEADS * MEM_HEAD_DIM
N_BRANCH = 3
LN_EPS = 1e-5
NEG_INF = -1e30
LOG2E = math.log2(math.e)

LANES = 128
VMEM_LIMIT = 56 * 1024 * 1024
CUMSUM_BLOCK = 256
CONV_ROW_GROUP = 512
SCORES_AHEAD = 2
F32 = jnp.float32
BF16 = jnp.bfloat16

IN_SIZES = (FOX_WIDTH, FOX_WIDTH, FOX_WIDTH, FOX_HEADS, FOX_WIDTH,
            CONV_WIDTH, CONV_WIDTH, CONV_WIDTH, CONV_WIDTH, MEM_WIDTH, MEM_WIDTH)
IN_OFFS = tuple(sum(IN_SIZES[:i]) for i in range(len(IN_SIZES) + 1))

_NT = (((1,), (1,)), ((), ()))


def _params(*sem):
    return pltpu.CompilerParams(dimension_semantics=sem, vmem_limit_bytes=VMEM_LIMIT)


def _silu(g):
    return g / (1.0 + jnp.exp(-g))


def _sigmoid(g):
    return 1.0 / (1.0 + jnp.exp(-g))


def _proj_kernel(x_ref, w_ref, o_ref, *, scale):
    acc = lax.dot_general(x_ref[...], w_ref[...], _NT, preferred_element_type=F32)
    if scale != 1.0:
        acc = acc * scale
    o_ref[...] = acc.astype(o_ref.dtype)


def _project(x16, w16, col0, n, scale, out_dtype, tm, tn, name):
    m, k = x16.shape
    assert col0 % tn == 0 and n % tn == 0
    j0 = col0 // tn
    return pl.pallas_call(
        functools.partial(_proj_kernel, scale=scale),
        grid=(m // tm, n // tn),
        in_specs=[pl.BlockSpec((tm, k), lambda i, j: (i, 0)),
                  pl.BlockSpec((tn, k), lambda i, j: (j0 + j, 0))],
        out_specs=pl.BlockSpec((tm, tn), lambda i, j: (i, j)),
        out_shape=jax.ShapeDtypeStruct((m, n), out_dtype),
        compiler_params=_params("parallel", "parallel"),
        name=name,
    )(x16, w16)


def _proj_heads_kernel(*refs, nb, extra):
    if extra == "first":
        x_ref, w_ref, wf_ref, o_ref, x16_ref, zf_ref = refs
        x16 = x_ref[...].astype(BF16)
        x16_ref[...] = x16
        zf_ref[...] = lax.dot_general(x16, wf_ref[...], _NT, preferred_element_type=F32)
    else:
        x_ref, w_ref, o_ref, *extra_refs = refs
        x16 = x_ref[...]
    acc = lax.dot_general(x16, w_ref[...], _NT, preferred_element_type=F32)
    t = x_ref.shape[0] // nb
    for b in range(nb):
        for h in range(FOX_HEADS):
            tile = acc[b * t:(b + 1) * t, h * LANES:(h + 1) * LANES]
            o_ref[b, h] = tile
            if extra == "transposed":
                extra_refs[0][b, h] = tile.T.astype(BF16)


def _project_heads(x, w16, col0, batch, tm, extra, name, wf16=None):
    m, k = x.shape
    t = m // batch
    j0 = col0 // FOX_WIDTH
    if t >= tm:
        nb, tb, per = 1, tm, t // tm
        o_map = lambda i: (i // per, 0, i % per, 0)
        t_map = lambda i: (i // per, 0, 0, i % per)
    else:
        nb, tb = tm // t, t
        o_map = t_map = lambda i: (i, 0, 0, 0)
    out_specs = [pl.BlockSpec((nb, FOX_HEADS, tb, LANES), o_map)]
    out_shape = [jax.ShapeDtypeStruct((batch, FOX_HEADS, t, LANES), F32)]
    in_specs = [pl.BlockSpec((tm, k), lambda i: (i, 0)),
                pl.BlockSpec((FOX_WIDTH, k), lambda i: (j0, 0))]
    args = [x, w16]
    if extra == "first":
        in_specs.append(pl.BlockSpec((LANES, k), lambda i: (0, 0)))
        args.append(wf16)
        out_specs += [pl.BlockSpec((tm, k), lambda i: (i, 0)), pl.BlockSpec((tm, LANES), lambda i: (i, 0))]
        out_shape += [jax.ShapeDtypeStruct((m, k), BF16), jax.ShapeDtypeStruct((m, LANES), F32)]
    elif extra == "transposed":
        out_specs.append(pl.BlockSpec((nb, FOX_HEADS, LANES, tb), t_map))
        out_shape.append(jax.ShapeDtypeStruct((batch, FOX_HEADS, LANES, t), BF16))
    res = pl.pallas_call(
        functools.partial(_proj_heads_kernel, nb=nb, extra=extra),
        grid=(m // tm,),
        in_specs=in_specs,
        out_specs=out_specs,
        out_shape=out_shape,
        compiler_params=_params("parallel"),
        name=name,
    )(*args)
    return res if extra else res[0]


def _split3(x):
    hi = x.astype(BF16)
    rem = x - hi.astype(F32)
    mid = rem.astype(BF16)
    lo = (rem - mid.astype(F32)).astype(BF16)
    return hi, mid, lo


def _cumsum_rows(x):
    s = x.shape[0]
    r = lax.broadcasted_iota(jnp.int32, (CUMSUM_BLOCK, CUMSUM_BLOCK), 0)
    c = lax.broadcasted_iota(jnp.int32, (CUMSUM_BLOCK, CUMSUM_BLOCK), 1)
    tri = (r >= c).astype(BF16)
    carry = jnp.zeros((1, x.shape[1]), F32)
    out = []
    for start in range(0, s, CUMSUM_BLOCK):
        n = min(CUMSUM_BLOCK, s - start)
        hi, mid, lo = _split3(x[start:start + n])
        t = tri[:n, :n]
        cb = (jnp.dot(t, hi, preferred_element_type=F32)
              + jnp.dot(t, mid, preferred_element_type=F32)
              + jnp.dot(t, lo, preferred_element_type=F32)) + carry
        carry = cb[n - 1:n]
        out.append(cb)
    return jnp.concatenate(out, axis=0) if len(out) > 1 else out[0]


def _fox_prep_kernel(*refs, past_len):
    if past_len:
        zf_ref, bf_ref, past_ref, logf_ref, c3_ref = refs
    else:
        zf_ref, bf_ref, logf_ref, c3_ref = refs
    z = zf_ref[...] + bf_ref[...]
    logf = jnp.minimum(z, 0.0) - jnp.log1p(jnp.exp(-jnp.abs(z)))
    logf_ref[...] = logf
    full = jnp.concatenate([past_ref[0], logf], axis=0) if past_len else logf
    lane = lax.broadcasted_iota(jnp.int32, (1, LANES), 1)
    cum = jnp.where(lane < FOX_HEADS, _cumsum_rows(full) * LOG2E, 0.0)
    hi, mid, lo = (p.astype(F32) for p in _split3(cum))
    packed = hi + pltpu.roll(mid, FOX_HEADS, 1) + pltpu.roll(lo, 2 * FOX_HEADS, 1)
    c3_ref[...] = packed.astype(BF16)


def _fox_prep(zf, bf_pad, past_logf, batch):
    m = zf.shape[0]
    t = m // batch
    past_len = 0 if past_logf is None else past_logf.shape[1]
    sk = past_len + t
    in_specs = [pl.BlockSpec((t, LANES), lambda b: (b, 0)),
                pl.BlockSpec((1, LANES), lambda b: (0, 0))]
    args = [zf, bf_pad]
    if past_len:
        in_specs.append(pl.BlockSpec((1, past_len, LANES), lambda b: (b, 0, 0)))
        args.append(past_logf)
    return pl.pallas_call(
        functools.partial(_fox_prep_kernel, past_len=past_len),
        grid=(batch,),
        in_specs=in_specs,
        out_specs=[pl.BlockSpec((t, LANES), lambda b: (b, 0)),
                   pl.BlockSpec((sk, LANES), lambda b: (b, 0))],
        out_shape=[jax.ShapeDtypeStruct((m, LANES), F32),
                   jax.ShapeDtypeStruct((batch * sk, LANES), BF16)],
        compiler_params=_params("parallel"),
        name="fox_prep",
    )(*args)


def _bias_selectors():
    sel = np.zeros((FOX_HEADS, LANES, 2 * LANES), np.float32)
    for h in range(FOX_HEADS):
        for piece in range(3):
            sel[h, piece * FOX_HEADS + h, piece] = 1.0
            sel[h, piece * FOX_HEADS + h, LANES + 3 + piece] = -1.0
    ones = np.zeros((2, 1, LANES), np.float32)
    ones[0, 0, 3:6] = 1.0
    ones[1, 0, 0:3] = 1.0
    return jnp.asarray(sel, BF16), jnp.asarray(ones, F32)


def _fox_attn_t_kernel(q_ref, c3_ref, sel_ref, one_ref, k_ref, vt_ref, g_ref, o_ref, qq, kk, *, tq):
    t = q_ref.shape[0]
    aug = jnp.dot(c3_ref[...], sel_ref[0], preferred_element_type=F32)
    qq[:, :LANES] = q_ref[...]
    qq[:, LANES:] = (aug[:, :LANES] + one_ref[0]).astype(BF16)
    kk[:, :LANES] = k_ref[0, 0].astype(BF16)
    kk[:, LANES:] = (aug[:, LANES:] + one_ref[1]).astype(BF16)
    key = lax.broadcasted_iota(jnp.int32, (tq, tq), 0)
    qry = lax.broadcasted_iota(jnp.int32, (tq, tq), 1)
    causal = key <= qry
    def scores(r0):
        qi = qq[r0:r0 + tq, :]
        s_d = lax.dot_general(kk[r0:r0 + tq, :], qi, _NT, preferred_element_type=F32)
        s_p = lax.dot_general(kk[:r0, :], qi, _NT, preferred_element_type=F32) if r0 else None
        return jnp.where(causal, s_d, NEG_INF), s_p

    order = [j * tq for j in range(t // tq)]
    ahead = [scores(r) for r in order[:SCORES_AHEAD]]
    for i, r0 in enumerate(order):
        s_d, s_p = ahead.pop(0)
        if i + SCORES_AHEAD < len(order):
            ahead.append(scores(order[i + SCORES_AHEAD]))
        m = jnp.max(s_d, axis=0, keepdims=True)
        if r0:
            m = jnp.maximum(m, jnp.max(s_p, axis=0, keepdims=True))
            p_p = jnp.exp2(s_p - m)
        p_d = jnp.exp2(s_d - m)
        l = jnp.sum(p_d, axis=0, keepdims=True)
        o = jnp.dot(vt_ref[0, 0, :, r0:r0 + tq], p_d.astype(BF16), preferred_element_type=F32)
        if r0:
            l = l + jnp.sum(p_p, axis=0, keepdims=True)
            o = o + jnp.dot(vt_ref[0, 0, :, :r0], p_p.astype(BF16), preferred_element_type=F32)
        g = g_ref[r0:r0 + tq, :]
        o_ref[r0:r0 + tq, :] = ((o / l).T * _silu(g)).astype(o_ref.dtype)


def _fox_attn_t(q16, c3, sel, ones, k32, vt16, g_fox, tq):
    batch, _, t, _ = k32.shape
    tok = lambda b, h: (b, h)
    per_head = lambda b, h: (b, h, 0, 0)
    return pl.pallas_call(
        functools.partial(_fox_attn_t_kernel, tq=tq),
        grid=(batch, FOX_HEADS),
        in_specs=[pl.BlockSpec((t, LANES), tok),
                  pl.BlockSpec((t, LANES), lambda b, h: (b, 0)),
                  pl.BlockSpec((1, LANES, 2 * LANES), lambda b, h: (h, 0, 0)),
                  pl.BlockSpec((2, 1, LANES), lambda b, h: (0, 0, 0)),
                  pl.BlockSpec((1, 1, t, LANES), per_head),
                  pl.BlockSpec((1, 1, LANES, t), per_head),
                  pl.BlockSpec((t, LANES), tok)],
        out_specs=pl.BlockSpec((t, LANES), tok),
        out_shape=jax.ShapeDtypeStruct((batch * t, FOX_WIDTH), BF16),
        scratch_shapes=[pltpu.VMEM((t, 2 * LANES), BF16),
                        pltpu.VMEM((t, 2 * LANES), BF16)],
        compiler_params=_params("parallel", "parallel"),
        name="fox_attn_t",
    )(q16, c3, sel, ones, k32, vt16, g_fox)


def _fox_attn_hist_kernel(q_ref, c3_ref, sel_ref, one_ref, k_ref, v_ref, g_ref, kp_ref, vp_ref, o_ref,
                          qq, kk, vv):
    t = q_ref.shape[0]
    past_len = kp_ref.shape[2]
    c3 = c3_ref[...]
    for h in range(FOX_HEADS):
        cols = slice(h * LANES, (h + 1) * LANES)
        aug = jnp.dot(c3, sel_ref[h], preferred_element_type=F32)
        qq[h, :, :LANES] = q_ref[:, cols]
        qq[h, :, LANES:] = (aug[past_len:, :LANES] + one_ref[0]).astype(BF16)
        kk[h, :past_len, :LANES] = kp_ref[0, h].astype(BF16)
        kk[h, past_len:, :LANES] = k_ref[0, h].astype(BF16)
        kk[h, :, LANES:] = (aug[:, LANES:] + one_ref[1]).astype(BF16)
        vv[h, :past_len, :] = vp_ref[0, h].astype(BF16)
        vv[h, past_len:, :] = v_ref[0, h].astype(BF16)
    row = lax.broadcasted_iota(jnp.int32, (t, t), 0)
    col = lax.broadcasted_iota(jnp.int32, (t, t), 1)
    causal = col <= row
    scores = []
    for h in range(FOX_HEADS):
        s_p = lax.dot_general(qq[h], kk[h, :past_len, :], _NT, preferred_element_type=F32)
        s_d = lax.dot_general(qq[h], kk[h, past_len:, :], _NT, preferred_element_type=F32)
        scores.append((s_p, jnp.where(causal, s_d, NEG_INF)))
    for h in range(FOX_HEADS):
        cols = slice(h * LANES, (h + 1) * LANES)
        s_p, s_d = scores[h]
        m = jnp.maximum(jnp.max(s_p, axis=-1, keepdims=True), jnp.max(s_d, axis=-1, keepdims=True))
        p_p = jnp.exp2(s_p - m)
        p_d = jnp.exp2(s_d - m)
        l = jnp.sum(p_d, axis=-1, keepdims=True) + jnp.sum(p_p, axis=-1, keepdims=True)
        o = (jnp.dot(p_d.astype(BF16), vv[h, past_len:, :], preferred_element_type=F32)
             + jnp.dot(p_p.astype(BF16), vv[h, :past_len, :], preferred_element_type=F32))
        o_ref[:, cols] = (o / l * _silu(g_ref[:, cols])).astype(o_ref.dtype)


def _fox_attn_hist(q16, c3, sel, ones, k32, v32, g_fox, k_past, v_past):
    batch, _, t, _ = k32.shape
    past_len = k_past.shape[2]
    sk = past_len + t
    tok = lambda b: (b, 0)
    heads = lambda b: (b, 0, 0, 0)
    return pl.pallas_call(
        _fox_attn_hist_kernel,
        grid=(batch,),
        in_specs=[pl.BlockSpec((t, FOX_WIDTH), tok),
                  pl.BlockSpec((sk, LANES), tok),
                  pl.BlockSpec((FOX_HEADS, LANES, 2 * LANES), lambda b: (0, 0, 0)),
                  pl.BlockSpec((2, 1, LANES), lambda b: (0, 0, 0)),
                  pl.BlockSpec((1, FOX_HEADS, t, LANES), heads),
                  pl.BlockSpec((1, FOX_HEADS, t, LANES), heads),
                  pl.BlockSpec((t, FOX_WIDTH), tok),
                  pl.BlockSpec((1, FOX_HEADS, past_len, LANES), heads),
                  pl.BlockSpec((1, FOX_HEADS, past_len, LANES), heads)],
        out_specs=pl.BlockSpec((t, FOX_WIDTH), tok),
        out_shape=jax.ShapeDtypeStruct((batch * t, FOX_WIDTH), BF16),
        scratch_shapes=[pltpu.VMEM((FOX_HEADS, t, 2 * LANES), BF16),
                        pltpu.VMEM((FOX_HEADS, sk, 2 * LANES), BF16),
                        pltpu.VMEM((FOX_HEADS, sk, LANES), BF16)],
        compiler_params=_params("parallel"),
        name="fox_attn_hist",
    )(q16, c3, sel, ones, k32, v32, g_fox, k_past, v_past)


def _conv_kernel(x_ref, wb_ref, wc_ref, wh_ref, wg_ref, prev_ref, cw_ref, cb_ref, a_ref, tail_ref, *,
                 nb, group):
    rows_total = x_ref.shape[0]
    t = rows_total // nb
    tc = a_ref.shape[1]
    piece = min(group, t)
    w0, w1, w2 = cw_ref[0:1, :], cw_ref[1:2, :], cw_ref[2:3, :]
    bias = cb_ref[...]
    rowi = lax.broadcasted_iota(jnp.int32, (piece, tc), 0)

    def project(gi):
        x = x_ref[gi * group:(gi + 1) * group, :]
        zb = lax.dot_general(x, wb_ref[...], _NT, preferred_element_type=F32)
        zu = (lax.dot_general(x, wc_ref[...], _NT, preferred_element_type=F32)
              * lax.dot_general(x, wh_ref[...], _NT, preferred_element_type=F32))
        zg = lax.dot_general(x, wg_ref[...], _NT, preferred_element_type=F32)
        return zb, zu, zg

    def finish(gi, z, carry):
        zb, zu, zg = z
        for pi in range(group // piece):
            r0 = gi * group + pi * piece
            b, off = divmod(r0, t)
            rows = slice(pi * piece, (pi + 1) * piece)
            u = zu[rows]
            p0, p1 = (prev_ref[b, 0:1, :], prev_ref[b, 1:2, :]) if off == 0 else carry
            u1 = jnp.where(rowi == 0, p1, pltpu.roll(u, 1, 0))
            u2 = jnp.where(rowi == 0, p0, jnp.where(rowi == 1, p1, pltpu.roll(u, 2, 0)))
            c = w0 * u2 + w1 * u1 + w2 * u + bias
            a_ref[r0:r0 + piece, :] = (zb[rows] * c * _silu(zg[rows])).astype(a_ref.dtype)
            carry = (u[piece - 2:piece - 1, :], u[piece - 1:piece, :])
            if off + piece == t:
                tail_ref[b] = u[piece - (CONV_K - 1):, :]
        return carry

    n_groups = rows_total // group
    ahead = project(0)
    carry = None
    for gi in range(n_groups):
        z = ahead
        if gi + 1 < n_groups:
            ahead = project(gi + 1)
        carry = finish(gi, z, carry)


def _conv_branch(x16, wb16, prev, conv_w, conv_b, batch, nb, tc):
    m = x16.shape[0]
    t = m // batch
    wspec = lambda part: pl.BlockSpec((tc, D_MODEL), lambda c, b: (part * (CONV_WIDTH // tc) + c, 0))
    return pl.pallas_call(
        functools.partial(_conv_kernel, nb=nb, group=min(nb * t, CONV_ROW_GROUP)),
        grid=(CONV_WIDTH // tc, batch // nb),
        in_specs=[pl.BlockSpec((nb * t, D_MODEL), lambda c, b: (b, 0)),
                  wspec(0), wspec(1), wspec(2), wspec(3),
                  pl.BlockSpec((nb, CONV_K - 1, tc), lambda c, b: (b, 0, c)),
                  pl.BlockSpec((CONV_K, tc), lambda c, b: (0, c)),
                  pl.BlockSpec((1, tc), lambda c, b: (0, c))],
        out_specs=[pl.BlockSpec((nb * t, tc), lambda c, b: (b, c)),
                   pl.BlockSpec((nb, CONV_K - 1, tc), lambda c, b: (b, 0, c))],
        out_shape=[jax.ShapeDtypeStruct((m, CONV_WIDTH), BF16),
                   jax.ShapeDtypeStruct((batch, CONV_K - 1, CONV_WIDTH), F32)],
        compiler_params=_params("parallel", "parallel"),
        name="conv_branch",
    )(x16, wb16, wb16, wb16, wb16, prev, conv_w, conv_b)


def _mem_kv_kernel(x_ref, w_ref, o32_ref, o16_ref, *, nb):
    acc = jnp.dot(x_ref[...], w_ref[...], preferred_element_type=F32)
    o16_ref[...] = acc.astype(BF16)
    for b in range(nb):
        for h in range(MEM_HEADS):
            o32_ref[0, b, h] = acc[b * MEM_TOKENS:(b + 1) * MEM_TOKENS,
                                   h * MEM_HEAD_DIM:(h + 1) * MEM_HEAD_DIM]


def _mem_kv(mem16, w16, batch, nb):
    m, k = mem16.shape
    tm = nb * MEM_TOKENS
    return pl.pallas_call(
        functools.partial(_mem_kv_kernel, nb=nb),
        grid=(m // tm, 2),
        in_specs=[pl.BlockSpec((tm, k), lambda i, j: (i, 0)),
                  pl.BlockSpec((k, MEM_WIDTH), lambda i, j: (0, j))],
        out_specs=[pl.BlockSpec((1, nb, MEM_HEADS, MEM_TOKENS, MEM_HEAD_DIM), lambda i, j: (j, i, 0, 0, 0)),
                   pl.BlockSpec((tm, MEM_WIDTH), lambda i, j: (i, j))],
        out_shape=[jax.ShapeDtypeStruct((2, batch, MEM_HEADS, MEM_TOKENS, MEM_HEAD_DIM), F32),
                   jax.ShapeDtypeStruct((m, 2 * MEM_WIDTH), BF16)],
        compiler_params=_params("parallel", "parallel"),
        name="mem_kv",
    )(mem16, w16)


def _mem_branch_kernel(x_ref, w_ref, kv_ref, o_ref, *, nb):
    t = x_ref.shape[0] // nb
    x = x_ref[...]
    stages = [(b, h) for b in range(nb) for h in range(MEM_HEADS)]

    def project(h):
        cs = slice(h * MEM_HEAD_DIM, (h + 1) * MEM_HEAD_DIM)
        gs = slice(MEM_WIDTH + h * MEM_HEAD_DIM, MEM_WIDTH + (h + 1) * MEM_HEAD_DIM)
        q = lax.dot_general(x, w_ref[cs, :], _NT, preferred_element_type=F32) * MEM_HEAD_DIM ** -0.5
        g = lax.dot_general(x, w_ref[gs, :], _NT, preferred_element_type=F32)
        return q.astype(BF16), g

    def scores(stage, proj):
        b, h = stage
        cs = slice(h * MEM_HEAD_DIM, (h + 1) * MEM_HEAD_DIM)
        q = proj[h][0][b * t:(b + 1) * t]
        return lax.dot_general(q, kv_ref[b * MEM_TOKENS:(b + 1) * MEM_TOKENS, cs], _NT,
                               preferred_element_type=F32)

    proj = {0: project(0)}
    ahead = scores(stages[0], proj)
    for i, (b, h) in enumerate(stages):
        s = ahead
        if i + 1 < len(stages):
            nh = stages[i + 1][1]
            if nh not in proj:
                proj[nh] = project(nh)
            ahead = scores(stages[i + 1], proj)
        rows = slice(b * t, (b + 1) * t)
        vs = slice(MEM_WIDTH + h * MEM_HEAD_DIM, MEM_WIDTH + (h + 1) * MEM_HEAD_DIM)
        m = jnp.max(s, axis=-1, keepdims=True)
        p = jnp.exp(s - m)
        l = jnp.sum(p, axis=-1, keepdims=True)
        o = jnp.dot(p.astype(BF16), kv_ref[b * MEM_TOKENS:(b + 1) * MEM_TOKENS, vs],
                    preferred_element_type=F32)
        o_ref[rows, h * MEM_HEAD_DIM:(h + 1) * MEM_HEAD_DIM] = (
            o / l * _silu(proj[h][1][rows])).astype(o_ref.dtype)


def _mem_branch(x16, w16, mkv16, batch, tm):
    m = x16.shape[0]
    t = m // batch
    if t >= tm:
        nb, per = 1, t // tm
        kv_map = lambda i: (i // per, 0)
    else:
        nb = tm // t
        kv_map = lambda i: (i, 0)
    return pl.pallas_call(
        functools.partial(_mem_branch_kernel, nb=nb),
        grid=(m // tm,),
        in_specs=[pl.BlockSpec((tm, D_MODEL), lambda i: (i, 0)),
                  pl.BlockSpec((2 * MEM_WIDTH, D_MODEL), lambda i: (0, 0)),
                  pl.BlockSpec((nb * MEM_TOKENS, 2 * MEM_WIDTH), kv_map)],
        out_specs=pl.BlockSpec((tm, MEM_WIDTH), lambda i: (i, 0)),
        out_shape=jax.ShapeDtypeStruct((m, MEM_WIDTH), BF16),
        compiler_params=_params("parallel"),
        name="mem_branch",
    )(x16, w16, mkv16)


def _merge_kernel(af_ref, ac_ref, am_ref, x_ref, wf_ref, wc_ref, wm_ref,
                  g1w_ref, g2w_ref, g3w_ref, g1b_ref, g2b_ref, g3b_ref, o_ref):
    x = x_ref[...]
    z1 = jnp.dot(x, g1w_ref[...], preferred_element_type=F32)
    z2 = jnp.dot(x, g2w_ref[...], preferred_element_type=F32)
    z3 = jnp.dot(x, g3w_ref[...], preferred_element_type=F32)
    acc = _sigmoid(z1 + g1b_ref[...]) * jnp.dot(af_ref[...], wf_ref[...], preferred_element_type=F32)
    acc = acc + _sigmoid(z2 + g2b_ref[...]) * jnp.dot(ac_ref[...], wc_ref[...], preferred_element_type=F32)
    acc = acc + _sigmoid(z3 + g3b_ref[...]) * jnp.dot(am_ref[...], wm_ref[...], preferred_element_type=F32)
    o_ref[...] = acc.astype(o_ref.dtype)


def _merge(a_fox, a_conv, a_mem, x16, wfo, wco, wmo, wmerge, bmerge, tm, tn):
    m = x16.shape[0]
    nj = D_MODEL // tn
    row = lambda j, i: (i, 0)
    colb = lambda k: (lambda j, i: (0, j + k * nj))
    return pl.pallas_call(
        _merge_kernel,
        grid=(nj, m // tm),
        in_specs=[pl.BlockSpec((tm, FOX_WIDTH), row),
                  pl.BlockSpec((tm, CONV_WIDTH), row),
                  pl.BlockSpec((tm, MEM_WIDTH), row),
                  pl.BlockSpec((tm, D_MODEL), row),
                  pl.BlockSpec((FOX_WIDTH, tn), colb(0)),
                  pl.BlockSpec((CONV_WIDTH, tn), colb(0)),
                  pl.BlockSpec((MEM_WIDTH, tn), colb(0)),
                  pl.BlockSpec((D_MODEL, tn), colb(0)),
                  pl.BlockSpec((D_MODEL, tn), colb(1)),
                  pl.BlockSpec((D_MODEL, tn), colb(2)),
                  pl.BlockSpec((1, tn), colb(0)),
                  pl.BlockSpec((1, tn), colb(1)),
                  pl.BlockSpec((1, tn), colb(2))],
        out_specs=pl.BlockSpec((tm, tn), lambda j, i: (i, j)),
        out_shape=jax.ShapeDtypeStruct((m, D_MODEL), BF16),
        compiler_params=_params("parallel", "parallel"),
        name="merge",
    )(a_fox, a_conv, a_mem, x16, wfo, wco, wmo, wmerge, wmerge, wmerge, bmerge, bmerge, bmerge)


def _out_kernel(m_ref, w_ref, x_ref, g_ref, b_ref, o_ref, *, alpha):
    half = m_ref.shape[0] // 2
    hs = [jnp.dot(m_ref[i * half:(i + 1) * half, :], w_ref[...], preferred_element_type=F32)
          for i in range(2)]
    for i, h in enumerate(hs):
        rows = slice(i * half, (i + 1) * half)
        r = alpha * x_ref[rows, :] + h
        mu = jnp.mean(r, axis=-1, keepdims=True)
        d = r - mu
        var = jnp.mean(d * d, axis=-1, keepdims=True)
        o_ref[rows, :] = d * lax.rsqrt(var + LN_EPS) * g_ref[...] + b_ref[...]


def _out_proj_norm(m16, wo16, x32, ln_g, ln_b, alpha, tm):
    m = x32.shape[0]
    return pl.pallas_call(
        functools.partial(_out_kernel, alpha=alpha),
        grid=(m // tm,),
        in_specs=[pl.BlockSpec((tm, D_MODEL), lambda i: (i, 0)),
                  pl.BlockSpec((D_MODEL, D_MODEL), lambda i: (0, 0)),
                  pl.BlockSpec((tm, D_MODEL), lambda i: (i, 0)),
                  pl.BlockSpec((1, D_MODEL), lambda i: (0, 0)),
                  pl.BlockSpec((1, D_MODEL), lambda i: (0, 0))],
        out_specs=pl.BlockSpec((tm, D_MODEL), lambda i: (i, 0)),
        out_shape=jax.ShapeDtypeStruct((m, D_MODEL), F32),
        compiler_params=_params("parallel"),
        name="out_proj_norm",
    )(m16, wo16, x32, ln_g, ln_b)


def _layer(x32, batch, wts, alpha, mkv16, prev_conv, k_past, v_past, past_logf, tiles):
    tm = tiles["tm"]
    wa = wts["w_qkv"]
    k32, x16, zf = _project_heads(x32, wa, FOX_WIDTH, batch, tiles["kv_tm"], "first", "proj_k", wts["w_f"])
    if k_past is None:
        v32, vt16 = _project_heads(x16, wa, 2 * FOX_WIDTH, batch, tiles["kv_tm"], "transposed", "proj_v")
    else:
        v32 = _project_heads(x16, wa, 2 * FOX_WIDTH, batch, tiles["kv_tm"], None, "proj_v")
    q16 = _project(x16, wa, 0, FOX_WIDTH, FOX_HEAD_DIM ** -0.5 * LOG2E, BF16, tm, FOX_WIDTH, "proj_q")
    g_fox = _project(x16, wts["w_gfox"], 0, FOX_WIDTH, 1.0, F32, tm, FOX_WIDTH, "proj_gfox")
    logf, c3 = _fox_prep(zf, wts["b_f"], past_logf, batch)
    if k_past is None:
        a_fox = _fox_attn_t(q16, c3, wts["sel"], wts["sel_ones"], k32, vt16, g_fox, tiles["tq"])
    else:
        a_fox = _fox_attn_hist(q16, c3, wts["sel"], wts["sel_ones"], k32, v32, g_fox, k_past, v_past)
    a_conv, tail = _conv_branch(x16, wts["w_conv"], prev_conv, wts["conv_w"], wts["conv_b"],
                                batch, tiles["conv_nb"], tiles["conv_tc"])
    a_mem = _mem_branch(x16, wts["w_mem"], mkv16, batch, tiles["mem_tm"])
    m16 = _merge(a_fox, a_conv, a_mem, x16, wts["w_fox_out"], wts["w_conv_out"], wts["w_mem_out"],
                 wts["w_merge"], wts["b_merge"], tiles["merge_tm"], 512)
    y = _out_proj_norm(m16, wts["w_o"], x32, wts["ln_g"], wts["ln_b"], alpha, tiles["out_tm"])
    return y, k32, v32, logf, tail


def _layer_weights(l, w_in, fox_bf, conv_w, conv_b, w_mem_kv, w_fox_out, w_conv_out, w_mem_out,
                   w_merge, b_merge, w_o, ln_g, ln_b):
    wt = w_in[l].T.astype(BF16)
    part = lambda i, j: wt[IN_OFFS[i]:IN_OFFS[j]]
    sel, sel_ones = _bias_selectors()
    return {
        "w_qkv": part(0, 3),
        "w_gfox": part(4, 5),
        "w_conv": part(5, 9),
        "w_mem": part(9, 11),
        "w_f": jnp.pad(part(3, 4), ((0, LANES - FOX_HEADS), (0, 0))),
        "b_f": jnp.pad(fox_bf[l].astype(F32), (0, LANES - FOX_HEADS)).reshape(1, LANES),
        "sel": sel,
        "sel_ones": sel_ones,
        "conv_w": conv_w[l],
        "conv_b": conv_b[l].reshape(1, CONV_WIDTH),
        "w_mem_kv": w_mem_kv[l].astype(BF16),
        "w_fox_out": w_fox_out[l].astype(BF16),
        "w_conv_out": w_conv_out[l].astype(BF16),
        "w_mem_out": w_mem_out[l].astype(BF16),
        "w_merge": w_merge[l].astype(BF16),
        "b_merge": b_merge[l].reshape(1, N_BRANCH * D_MODEL),
        "w_o": w_o[l].astype(BF16),
        "ln_g": ln_g[l].reshape(1, D_MODEL),
        "ln_b": ln_b[l].reshape(1, D_MODEL),
    }


def _to_time_major(x):
    return jnp.transpose(x, (0, 2, 1, 3))


def kernel(x_prompt, x_sample, mem_prompt, cache_fox_k, cache_fox_v, cache_fox_logf, state_conv,
           cache_mem_k, cache_mem_v, w_in, fox_bf, conv_w, conv_b, w_mem_kv, w_fox_out, w_conv_out,
           w_mem_out, w_merge, b_merge, w_o, ln_g, ln_b):
    depth = w_in.shape[0]
    alpha = (2 * depth) ** 0.25
    bp, sp, _ = x_prompt.shape
    bs, ts, _ = x_sample.shape
    prompt_tiles = dict(tm=1024, kv_tm=512, tq=256, conv_nb=1, conv_tc=256, mem_tm=1024, merge_tm=512,
                        out_tm=512)
    sample_tiles = dict(tm=bs * ts, kv_tm=bs * ts, tq=ts, conv_nb=bs, conv_tc=256, mem_tm=bs * ts,
                        merge_tm=bs * ts, out_tm=bs * ts)

    hp = x_prompt.reshape(bp * sp, D_MODEL)
    hs = x_sample.reshape(bs * ts, D_MODEL)
    mem16 = mem_prompt.reshape(bp * MEM_TOKENS, D_MODEL).astype(BF16)
    outs = [[] for _ in range(10)]
    for l in range(depth):
        wts = _layer_weights(l, w_in, fox_bf, conv_w, conv_b, w_mem_kv, w_fox_out, w_conv_out,
                             w_mem_out, w_merge, b_merge, w_o, ln_g, ln_b)
        mkv32, mkv16 = _mem_kv(mem16, wts["w_mem_kv"], bp, 4)
        prev0 = jnp.zeros((bp, CONV_K - 1, CONV_WIDTH), F32)
        hp, k_p, v_p, lf_p, tail_p = _layer(hp, bp, wts, alpha, mkv16, prev0, None, None, None,
                                            prompt_tiles)
        ck = jnp.transpose(cache_fox_k[l], (0, 2, 1, 3))
        cv = jnp.transpose(cache_fox_v[l], (0, 2, 1, 3))
        clf = jnp.pad(cache_fox_logf[l].astype(F32), ((0, 0), (0, 0), (0, LANES - FOX_HEADS)))
        cmkv16 = jnp.concatenate([cache_mem_k[l].reshape(bs * MEM_TOKENS, MEM_WIDTH),
                                  cache_mem_v[l].reshape(bs * MEM_TOKENS, MEM_WIDTH)], axis=1).astype(BF16)
        hs, k_s, v_s, lf_s, tail_s = _layer(hs, bs, wts, alpha, cmkv16, state_conv[l].astype(F32),
                                            ck, cv, clf, sample_tiles)
        vals = (_to_time_major(k_p), _to_time_major(v_p),
                lf_p[:, :FOX_HEADS].reshape(bp, sp, FOX_HEADS), tail_p,
                _to_time_major(mkv32[0]), _to_time_major(mkv32[1]),
                _to_time_major(k_s), _to_time_major(v_s),
                lf_s[:, :FOX_HEADS].reshape(bs, ts, FOX_HEADS), tail_s)
        for acc, val in zip(outs, vals):
            acc.append(val)
    return (hp.reshape(bp, sp, D_MODEL), hs.reshape(bs, ts, D_MODEL)) + tuple(jnp.stack(o) for o in outs)
```

```python
import functools
import math

import numpy as np
import jax
import jax.numpy as jnp
from jax import lax
from jax.experimental import pallas as pl
from jax.experimental.pallas import tpu as pltpu

D_MODEL = 2048
FOX_HEADS = 12
FOX_HEAD_DIM = 128
FOX_WIDTH = FOX_HEADS * FOX_HEAD_DIM
CONV_WIDTH = 1536
CONV_K = 3
MEM_TOKENS = 256
MEM_HEADS = 4
MEM_HEAD_DIM = 256
MEM_WIDTH = MEM_HEADS * MEM_HEAD_DIM
N_BRANCH = 3
LN_EPS = 1e-5
NEG_INF = -1e30
LOG2E = math.log2(math.e)

LANES = 128
VMEM_LIMIT = 56 * 1024 * 1024
CUMSUM_BLOCK = 256
CONV_ROW_GROUP = 512
SCORES_AHEAD = 2
F32 = jnp.float32
BF16 = jnp.bfloat16

IN_SIZES = (FOX_WIDTH, FOX_WIDTH, FOX_WIDTH, FOX_HEADS, FOX_WIDTH,
            CONV_WIDTH, CONV_WIDTH, CONV_WIDTH, CONV_WIDTH, MEM_WIDTH, MEM_WIDTH)
IN_OFFS = tuple(sum(IN_SIZES[:i]) for i in range(len(IN_SIZES) + 1))

_NT = (((1,), (1,)), ((), ()))


def _params(*sem):
    return pltpu.CompilerParams(dimension_semantics=sem, vmem_limit_bytes=VMEM_LIMIT)


def _silu(g):
    return g / (1.0 + jnp.exp(-g))


def _sigmoid(g):
    return 1.0 / (1.0 + jnp.exp(-g))


def _proj_kernel(x_ref, w_ref, o_ref, *, scale):
    acc = lax.dot_general(x_ref[...], w_ref[...], _NT, preferred_element_type=F32)
    if scale != 1.0:
        acc = acc * scale
    o_ref[...] = acc.astype(o_ref.dtype)


def _project(x16, w16, col0, n, scale, out_dtype, tm, tn, name):
    m, k = x16.shape
    assert col0 % tn == 0 and n % tn == 0
    j0 = col0 // tn
    return pl.pallas_call(
        functools.partial(_proj_kernel, scale=scale),
        grid=(m // tm, n // tn),
        in_specs=[pl.BlockSpec((tm, k), lambda i, j: (i, 0)),
                  pl.BlockSpec((tn, k), lambda i, j: (j0 + j, 0))],
        out_specs=pl.BlockSpec((tm, tn), lambda i, j: (i, j)),
        out_shape=jax.ShapeDtypeStruct((m, n), out_dtype),
        compiler_params=_params("parallel", "parallel"),
        name=name,
    )(x16, w16)


def _cast_kernel(x_ref, wf_ref, x16_ref, zf_ref):
    x16 = x_ref[...].astype(BF16)
    x16_ref[...] = x16
    zf_ref[...] = lax.dot_general(x16, wf_ref[...], _NT, preferred_element_type=F32)


def _cast_and_gate_logits(x32, wf16, tm):
    m, k = x32.shape
    return pl.pallas_call(
        _cast_kernel,
        grid=(m // tm,),
        in_specs=[pl.BlockSpec((tm, k), lambda i: (i, 0)),
                  pl.BlockSpec((LANES, k), lambda i: (0, 0))],
        out_specs=[pl.BlockSpec((tm, k), lambda i: (i, 0)), pl.BlockSpec((tm, LANES), lambda i: (i, 0))],
        out_shape=[jax.ShapeDtypeStruct((m, k), BF16), jax.ShapeDtypeStruct((m, LANES), F32)],
        compiler_params=_params("parallel"),
        name="cast_x",
    )(x32, wf16)


def _proj_heads_kernel(x_ref, w_ref, o_ref, *, nb):
    acc = lax.dot_general(x_ref[...], w_ref[...], _NT, preferred_element_type=F32)
    t = x_ref.shape[0] // nb
    for b in range(nb):
        for h in range(FOX_HEADS):
            o_ref[b, h] = acc[b * t:(b + 1) * t, h * LANES:(h + 1) * LANES]


def _project_heads(x, w16, col0, batch, tm, name):
    m, k = x.shape
    t = m // batch
    j0 = col0 // FOX_WIDTH
    if t >= tm:
        nb, tb, per = 1, tm, t // tm
        o_map = lambda i: (i // per, 0, i % per, 0)
    else:
        nb, tb = tm // t, t
        o_map = lambda i: (i, 0, 0, 0)
    return pl.pallas_call(
        functools.partial(_proj_heads_kernel, nb=nb),
        grid=(m // tm,),
        in_specs=[pl.BlockSpec((tm, k), lambda i: (i, 0)),
                  pl.BlockSpec((FOX_WIDTH, k), lambda i: (j0, 0))],
        out_specs=pl.BlockSpec((nb, FOX_HEADS, tb, LANES), o_map),
        out_shape=jax.ShapeDtypeStruct((batch, FOX_HEADS, t, LANES), F32),
        compiler_params=_params("parallel"),
        name=name,
    )(x, w16)


def _split3(x):
    hi = x.astype(BF16)
    rem = x - hi.astype(F32)
    mid = rem.astype(BF16)
    lo = (rem - mid.astype(F32)).astype(BF16)
    return hi, mid, lo


def _cumsum_rows(x):
    s = x.shape[0]
    r = lax.broadcasted_iota(jnp.int32, (CUMSUM_BLOCK, CUMSUM_BLOCK), 0)
    c = lax.broadcasted_iota(jnp.int32, (CUMSUM_BLOCK, CUMSUM_BLOCK), 1)
    tri = (r >= c).astype(BF16)
    carry = jnp.zeros((1, x.shape[1]), F32)
    out = []
    for start in range(0, s, CUMSUM_BLOCK):
        n = min(CUMSUM_BLOCK, s - start)
        hi, mid, lo = _split3(x[start:start + n])
        t = tri[:n, :n]
        cb = (jnp.dot(t, hi, preferred_element_type=F32)
              + jnp.dot(t, mid, preferred_element_type=F32)
              + jnp.dot(t, lo, preferred_element_type=F32)) + carry
        carry = cb[n - 1:n]
        out.append(cb)
    return jnp.concatenate(out, axis=0) if len(out) > 1 else out[0]


def _fox_prep_kernel(*refs, past_len):
    if past_len:
        zf_ref, bf_ref, past_ref, logf_ref, c3_ref = refs
    else:
        zf_ref, bf_ref, logf_ref, c3_ref = refs
    z = zf_ref[...] + bf_ref[...]
    logf = jnp.minimum(z, 0.0) - jnp.log1p(jnp.exp(-jnp.abs(z)))
    logf_ref[...] = logf
    full = jnp.concatenate([past_ref[0], logf], axis=0) if past_len else logf
    lane = lax.broadcasted_iota(jnp.int32, (1, LANES), 1)
    cum = jnp.where(lane < FOX_HEADS, _cumsum_rows(full) * LOG2E, 0.0)
    hi, mid, lo = (p.astype(F32) for p in _split3(cum))
    packed = hi + pltpu.roll(mid, FOX_HEADS, 1) + pltpu.roll(lo, 2 * FOX_HEADS, 1)
    c3_ref[...] = packed.astype(BF16)


def _fox_prep(zf, bf_pad, past_logf, batch):
    m = zf.shape[0]
    t = m // batch
    past_len = 0 if past_logf is None else past_logf.shape[1]
    sk = past_len + t
    in_specs = [pl.BlockSpec((t, LANES), lambda b: (b, 0)),
                pl.BlockSpec((1, LANES), lambda b: (0, 0))]
    args = [zf, bf_pad]
    if past_len:
        in_specs.append(pl.BlockSpec((1, past_len, LANES), lambda b: (b, 0, 0)))
        args.append(past_logf)
    return pl.pallas_call(
        functools.partial(_fox_prep_kernel, past_len=past_len),
        grid=(batch,),
        in_specs=in_specs,
        out_specs=[pl.BlockSpec((t, LANES), lambda b: (b, 0)),
                   pl.BlockSpec((sk, LANES), lambda b: (b, 0))],
        out_shape=[jax.ShapeDtypeStruct((m, LANES), F32),
                   jax.ShapeDtypeStruct((batch * sk, LANES), BF16)],
        compiler_params=_params("parallel"),
        name="fox_prep",
    )(*args)


def _bias_selectors():
    sel = np.zeros((FOX_HEADS, LANES, 2 * LANES), np.float32)
    for h in range(FOX_HEADS):
        for piece in range(3):
            sel[h, piece * FOX_HEADS + h, piece] = 1.0
            sel[h, piece * FOX_HEADS + h, LANES + 3 + piece] = -1.0
    ones = np.zeros((2, 1, LANES), np.float32)
    ones[0, 0, 3:6] = 1.0
    ones[1, 0, 0:3] = 1.0
    return jnp.asarray(sel, BF16), jnp.asarray(ones, F32)


def _fox_branch_kernel(x_ref, wq_ref, wk_ref, wv_ref, wg_ref, c3_ref, sel_ref, one_ref,
                       k_out, v_out, o_ref, wcat, qq, kk, vt, *, tq):
    t = x_ref.shape[0]
    for i, w_ref in enumerate((wq_ref, wk_ref, wv_ref, wg_ref)):
        wcat[i * LANES:(i + 1) * LANES, :] = w_ref[...]
    z = lax.dot_general(x_ref[...], wcat[...], _NT, preferred_element_type=F32)
    k, v, gate = z[:, LANES:2 * LANES], z[:, 2 * LANES:3 * LANES], z[:, 3 * LANES:]
    k_out[0, 0] = k
    v_out[0, 0] = v
    aug = jnp.dot(c3_ref[...], sel_ref[0], preferred_element_type=F32)
    qq[:, :LANES] = (z[:, :LANES] * (FOX_HEAD_DIM ** -0.5 * LOG2E)).astype(BF16)
    qq[:, LANES:] = (aug[:, :LANES] + one_ref[0]).astype(BF16)
    kk[:, :LANES] = k.astype(BF16)
    kk[:, LANES:] = (aug[:, LANES:] + one_ref[1]).astype(BF16)
    vt[...] = v.T.astype(BF16)
    key = lax.broadcasted_iota(jnp.int32, (tq, tq), 0)
    qry = lax.broadcasted_iota(jnp.int32, (tq, tq), 1)
    causal = key <= qry
    def scores(r0):
        qi = qq[r0:r0 + tq, :]
        s_d = lax.dot_general(kk[r0:r0 + tq, :], qi, _NT, preferred_element_type=F32)
        s_p = lax.dot_general(kk[:r0, :], qi, _NT, preferred_element_type=F32) if r0 else None
        return jnp.where(causal, s_d, NEG_INF), s_p

    order = [j * tq for j in range(t // tq)]
    ahead = [scores(r) for r in order[:SCORES_AHEAD]]
    for i, r0 in enumerate(order):
        s_d, s_p = ahead.pop(0)
        if i + SCORES_AHEAD < len(order):
            ahead.append(scores(order[i + SCORES_AHEAD]))
        m = jnp.max(s_d, axis=0, keepdims=True)
        if r0:
            m = jnp.maximum(m, jnp.max(s_p, axis=0, keepdims=True))
            p_p = jnp.exp2(s_p - m)
        p_d = jnp.exp2(s_d - m)
        l = jnp.sum(p_d, axis=0, keepdims=True)
        o = jnp.dot(vt[:, r0:r0 + tq], p_d.astype(BF16), preferred_element_type=F32)
        if r0:
            l = l + jnp.sum(p_p, axis=0, keepdims=True)
            o = o + jnp.dot(vt[:, :r0], p_p.astype(BF16), preferred_element_type=F32)
        o_ref[r0:r0 + tq, :] = ((o / l).T * _silu(gate[r0:r0 + tq, :])).astype(o_ref.dtype)


def _fox_branch(x16, w_qkv, w_gfox, c3, sel, ones, batch, tq):
    m = x16.shape[0]
    t = m // batch
    head_rows = lambda part: pl.BlockSpec((LANES, D_MODEL), lambda b, h: (part * FOX_HEADS + h, 0))
    per_head = pl.BlockSpec((1, 1, t, LANES), lambda b, h: (b, h, 0, 0))
    kv_shape = jax.ShapeDtypeStruct((batch, FOX_HEADS, t, LANES), F32)
    k32, v32, a_fox = pl.pallas_call(
        functools.partial(_fox_branch_kernel, tq=tq),
        grid=(batch, FOX_HEADS),
        in_specs=[pl.BlockSpec((t, D_MODEL), lambda b, h: (b, 0)),
                  head_rows(0), head_rows(1), head_rows(2), head_rows(0),
                  pl.BlockSpec((t, LANES), lambda b, h: (b, 0)),
                  pl.BlockSpec((1, LANES, 2 * LANES), lambda b, h: (h, 0, 0)),
                  pl.BlockSpec((2, 1, LANES), lambda b, h: (0, 0, 0))],
        out_specs=[per_head, per_head, pl.BlockSpec((t, LANES), lambda b, h: (b, h))],
        out_shape=[kv_shape, kv_shape, jax.ShapeDtypeStruct((m, FOX_WIDTH), BF16)],
        scratch_shapes=[pltpu.VMEM((4 * LANES, D_MODEL), BF16),
                        pltpu.VMEM((t, 2 * LANES), BF16),
                        pltpu.VMEM((t, 2 * LANES), BF16),
                        pltpu.VMEM((LANES, t), BF16)],
        compiler_params=_params("parallel", "parallel"),
        name="fox_branch",
    )(x16, w_qkv, w_qkv, w_qkv, w_gfox, c3, sel, ones)
    return a_fox, k32, v32


def _fox_attn_hist_kernel(q_ref, c3_ref, sel_ref, one_ref, k_ref, v_ref, g_ref, kp_ref, vp_ref, o_ref,
                          qq, kk, vv):
    t = q_ref.shape[0]
    past_len = kp_ref.shape[2]
    c3 = c3_ref[...]
    for h in range(FOX_HEADS):
        cols = slice(h * LANES, (h + 1) * LANES)
        aug = jnp.dot(c3, sel_ref[h], preferred_element_type=F32)
        qq[h, :, :LANES] = q_ref[:, cols]
        qq[h, :, LANES:] = (aug[past_len:, :LANES] + one_ref[0]).astype(BF16)
        kk[h, :past_len, :LANES] = kp_ref[0, h].astype(BF16)
        kk[h, past_len:, :LANES] = k_ref[0, h].astype(BF16)
        kk[h, :, LANES:] = (aug[:, LANES:] + one_ref[1]).astype(BF16)
        vv[h, :past_len, :] = vp_ref[0, h].astype(BF16)
        vv[h, past_len:, :] = v_ref[0, h].astype(BF16)
    row = lax.broadcasted_iota(jnp.int32, (t, t), 0)
    col = lax.broadcasted_iota(jnp.int32, (t, t), 1)
    causal = col <= row
    scores = []
    for h in range(FOX_HEADS):
        s_p = lax.dot_general(qq[h], kk[h, :past_len, :], _NT, preferred_element_type=F32)
        s_d = lax.dot_general(qq[h], kk[h, past_len:, :], _NT, preferred_element_type=F32)
        scores.append((s_p, jnp.where(causal, s_d, NEG_INF)))
    for h in range(FOX_HEADS):
        cols = slice(h * LANES, (h + 1) * LANES)
        s_p, s_d = scores[h]
        m = jnp.maximum(jnp.max(s_p, axis=-1, keepdims=True), jnp.max(s_d, axis=-1, keepdims=True))
        p_p = jnp.exp2(s_p - m)
        p_d = jnp.exp2(s_d - m)
        l = jnp.sum(p_d, axis=-1, keepdims=True) + jnp.sum(p_p, axis=-1, keepdims=True)
        o = (jnp.dot(p_d.astype(BF16), vv[h, past_len:, :], preferred_element_type=F32)
             + jnp.dot(p_p.astype(BF16), vv[h, :past_len, :], preferred_element_type=F32))
        o_ref[:, cols] = (o / l * _silu(g_ref[:, cols])).astype(o_ref.dtype)


def _fox_attn_hist(q16, c3, sel, ones, k32, v32, g_fox, k_past, v_past):
    batch, _, t, _ = k32.shape
    past_len = k_past.shape[2]
    sk = past_len + t
    tok = lambda b: (b, 0)
    heads = lambda b: (b, 0, 0, 0)
    return pl.pallas_call(
        _fox_attn_hist_kernel,
        grid=(batch,),
        in_specs=[pl.BlockSpec((t, FOX_WIDTH), tok),
                  pl.BlockSpec((sk, LANES), tok),
                  pl.BlockSpec((FOX_HEADS, LANES, 2 * LANES), lambda b: (0, 0, 0)),
                  pl.BlockSpec((2, 1, LANES), lambda b: (0, 0, 0)),
                  pl.BlockSpec((1, FOX_HEADS, t, LANES), heads),
                  pl.BlockSpec((1, FOX_HEADS, t, LANES), heads),
                  pl.BlockSpec((t, FOX_WIDTH), tok),
                  pl.BlockSpec((1, FOX_HEADS, past_len, LANES), heads),
                  pl.BlockSpec((1, FOX_HEADS, past_len, LANES), heads)],
        out_specs=pl.BlockSpec((t, FOX_WIDTH), tok),
        out_shape=jax.ShapeDtypeStruct((batch * t, FOX_WIDTH), BF16),
        scratch_shapes=[pltpu.VMEM((FOX_HEADS, t, 2 * LANES), BF16),
                        pltpu.VMEM((FOX_HEADS, sk, 2 * LANES), BF16),
                        pltpu.VMEM((FOX_HEADS, sk, LANES), BF16)],
        compiler_params=_params("parallel"),
        name="fox_attn_hist",
    )(q16, c3, sel, ones, k32, v32, g_fox, k_past, v_past)


def _conv_kernel(x_ref, wb_ref, wc_ref, wh_ref, wg_ref, prev_ref, cw_ref, cb_ref, a_ref, tail_ref, *,
                 nb, group):
    rows_total = x_ref.shape[0]
    t = rows_total // nb
    tc = a_ref.shape[1]
    piece = min(group, t)
    w0, w1, w2 = cw_ref[0:1, :], cw_ref[1:2, :], cw_ref[2:3, :]
    bias = cb_ref[...]
    rowi = lax.broadcasted_iota(jnp.int32, (piece, tc), 0)

    def project(gi):
        x = x_ref[gi * group:(gi + 1) * group, :]
        zb = lax.dot_general(x, wb_ref[...], _NT, preferred_element_type=F32)
        zu = (lax.dot_general(x, wc_ref[...], _NT, preferred_element_type=F32)
              * lax.dot_general(x, wh_ref[...], _NT, preferred_element_type=F32))
        zg = lax.dot_general(x, wg_ref[...], _NT, preferred_element_type=F32)
        return zb, zu, zg

    def finish(gi, z, carry):
        zb, zu, zg = z
        for pi in range(group // piece):
            r0 = gi * group + pi * piece
            b, off = divmod(r0, t)
            rows = slice(pi * piece, (pi + 1) * piece)
            u = zu[rows]
            p0, p1 = (prev_ref[b, 0:1, :], prev_ref[b, 1:2, :]) if off == 0 else carry
            u1 = jnp.where(rowi == 0, p1, pltpu.roll(u, 1, 0))
            u2 = jnp.where(rowi == 0, p0, jnp.where(rowi == 1, p1, pltpu.roll(u, 2, 0)))
            c = w0 * u2 + w1 * u1 + w2 * u + bias
            a_ref[r0:r0 + piece, :] = (zb[rows] * c * _silu(zg[rows])).astype(a_ref.dtype)
            carry = (u[piece - 2:piece - 1, :], u[piece - 1:piece, :])
            if off + piece == t:
                tail_ref[b] = u[piece - (CONV_K - 1):, :]
        return carry

    n_groups = rows_total // group
    ahead = project(0)
    carry = None
    for gi in range(n_groups):
        z = ahead
        if gi + 1 < n_groups:
            ahead = project(gi + 1)
        carry = finish(gi, z, carry)


def _conv_branch(x16, wb16, prev, conv_w, conv_b, batch, nb, tc):
    m = x16.shape[0]
    t = m // batch
    wspec = lambda part: pl.BlockSpec((tc, D_MODEL), lambda c, b: (part * (CONV_WIDTH // tc) + c, 0))
    return pl.pallas_call(
        functools.partial(_conv_kernel, nb=nb, group=min(nb * t, CONV_ROW_GROUP)),
        grid=(CONV_WIDTH // tc, batch // nb),
        in_specs=[pl.BlockSpec((nb * t, D_MODEL), lambda c, b: (b, 0)),
                  wspec(0), wspec(1), wspec(2), wspec(3),
                  pl.BlockSpec((nb, CONV_K - 1, tc), lambda c, b: (b, 0, c)),
                  pl.BlockSpec((CONV_K, tc), lambda c, b: (0, c)),
                  pl.BlockSpec((1, tc), lambda c, b: (0, c))],
        out_specs=[pl.BlockSpec((nb * t, tc), lambda c, b: (b, c)),
                   pl.BlockSpec((nb, CONV_K - 1, tc), lambda c, b: (b, 0, c))],
        out_shape=[jax.ShapeDtypeStruct((m, CONV_WIDTH), BF16),
                   jax.ShapeDtypeStruct((batch, CONV_K - 1, CONV_WIDTH), F32)],
        compiler_params=_params("parallel", "parallel"),
        name="conv_branch",
    )(x16, wb16, wb16, wb16, wb16, prev, conv_w, conv_b)


def _mem_kv_kernel(x_ref, w_ref, o32_ref, o16_ref, *, nb):
    acc = jnp.dot(x_ref[...], w_ref[...], preferred_element_type=F32)
    o16_ref[...] = acc.astype(BF16)
    for b in range(nb):
        for h in range(MEM_HEADS):
            o32_ref[0, b, h] = acc[b * MEM_TOKENS:(b + 1) * MEM_TOKENS,
                                   h * MEM_HEAD_DIM:(h + 1) * MEM_HEAD_DIM]


def _mem_kv(mem16, w16, batch, nb):
    m, k = mem16.shape
    tm = nb * MEM_TOKENS
    return pl.pallas_call(
        functools.partial(_mem_kv_kernel, nb=nb),
        grid=(m // tm, 2),
        in_specs=[pl.BlockSpec((tm, k), lambda i, j: (i, 0)),
                  pl.BlockSpec((k, MEM_WIDTH), lambda i, j: (0, j))],
        out_specs=[pl.BlockSpec((1, nb, MEM_HEADS, MEM_TOKENS, MEM_HEAD_DIM), lambda i, j: (j, i, 0, 0, 0)),
                   pl.BlockSpec((tm, MEM_WIDTH), lambda i, j: (i, j))],
        out_shape=[jax.ShapeDtypeStruct((2, batch, MEM_HEADS, MEM_TOKENS, MEM_HEAD_DIM), F32),
                   jax.ShapeDtypeStruct((m, 2 * MEM_WIDTH), BF16)],
        compiler_params=_params("parallel", "parallel"),
        name="mem_kv",
    )(mem16, w16)


def _mem_branch_kernel(x_ref, w_ref, kv_ref, o_ref, *, nb):
    t = x_ref.shape[0] // nb
    x = x_ref[...]
    stages = [(b, h) for b in range(nb) for h in range(MEM_HEADS)]

    def project(h):
        cs = slice(h * MEM_HEAD_DIM, (h + 1) * MEM_HEAD_DIM)
        gs = slice(MEM_WIDTH + h * MEM_HEAD_DIM, MEM_WIDTH + (h + 1) * MEM_HEAD_DIM)
        q = lax.dot_general(x, w_ref[cs, :], _NT, preferred_element_type=F32) * MEM_HEAD_DIM ** -0.5
        g = lax.dot_general(x, w_ref[gs, :], _NT, preferred_element_type=F32)
        return q.astype(BF16), g

    def scores(stage, proj):
        b, h = stage
        cs = slice(h * MEM_HEAD_DIM, (h + 1) * MEM_HEAD_DIM)
        q = proj[h][0][b * t:(b + 1) * t]
        return lax.dot_general(q, kv_ref[b * MEM_TOKENS:(b + 1) * MEM_TOKENS, cs], _NT,
                               preferred_element_type=F32)

    proj = {0: project(0)}
    ahead = scores(stages[0], proj)
    for i, (b, h) in enumerate(stages):
        s = ahead
        if i + 1 < len(stages):
            nh = stages[i + 1][1]
            if nh not in proj:
                proj[nh] = project(nh)
            ahead = scores(stages[i + 1], proj)
        rows = slice(b * t, (b + 1) * t)
        vs = slice(MEM_WIDTH + h * MEM_HEAD_DIM, MEM_WIDTH + (h + 1) * MEM_HEAD_DIM)
        m = jnp.max(s, axis=-1, keepdims=True)
        p = jnp.exp(s - m)
        l = jnp.sum(p, axis=-1, keepdims=True)
        o = jnp.dot(p.astype(BF16), kv_ref[b * MEM_TOKENS:(b + 1) * MEM_TOKENS, vs],
                    preferred_element_type=F32)
        o_ref[rows, h * MEM_HEAD_DIM:(h + 1) * MEM_HEAD_DIM] = (
            o / l * _silu(proj[h][1][rows])).astype(o_ref.dtype)


def _mem_branch(x16, w16, mkv16, batch, tm):
    m = x16.shape[0]
    t = m // batch
    if t >= tm:
        nb, per = 1, t // tm
        kv_map = lambda i: (i // per, 0)
    else:
        nb = tm // t
        kv_map = lambda i: (i, 0)
    return pl.pallas_call(
        functools.partial(_mem_branch_kernel, nb=nb),
        grid=(m // tm,),
        in_specs=[pl.BlockSpec((tm, D_MODEL), lambda i: (i, 0)),
                  pl.BlockSpec((2 * MEM_WIDTH, D_MODEL), lambda i: (0, 0)),
                  pl.BlockSpec((nb * MEM_TOKENS, 2 * MEM_WIDTH), kv_map)],
        out_specs=pl.BlockSpec((tm, MEM_WIDTH), lambda i: (i, 0)),
        out_shape=jax.ShapeDtypeStruct((m, MEM_WIDTH), BF16),
        compiler_params=_params("parallel"),
        name="mem_branch",
    )(x16, w16, mkv16)


def _merge_kernel(af_ref, ac_ref, am_ref, x_ref, wf_ref, wc_ref, wm_ref,
                  g1w_ref, g2w_ref, g3w_ref, g1b_ref, g2b_ref, g3b_ref, o_ref):
    x = x_ref[...]
    z1 = jnp.dot(x, g1w_ref[...], preferred_element_type=F32)
    z2 = jnp.dot(x, g2w_ref[...], preferred_element_type=F32)
    z3 = jnp.dot(x, g3w_ref[...], preferred_element_type=F32)
    acc = _sigmoid(z1 + g1b_ref[...]) * jnp.dot(af_ref[...], wf_ref[...], preferred_element_type=F32)
    acc = acc + _sigmoid(z2 + g2b_ref[...]) * jnp.dot(ac_ref[...], wc_ref[...], preferred_element_type=F32)
    acc = acc + _sigmoid(z3 + g3b_ref[...]) * jnp.dot(am_ref[...], wm_ref[...], preferred_element_type=F32)
    o_ref[...] = acc.astype(o_ref.dtype)


def _merge(a_fox, a_conv, a_mem, x16, wfo, wco, wmo, wmerge, bmerge, tm, tn):
    m = x16.shape[0]
    nj = D_MODEL // tn
    row = lambda j, i: (i, 0)
    colb = lambda k: (lambda j, i: (0, j + k * nj))
    return pl.pallas_call(
        _merge_kernel,
        grid=(nj, m // tm),
        in_specs=[pl.BlockSpec((tm, FOX_WIDTH), row),
                  pl.BlockSpec((tm, CONV_WIDTH), row),
                  pl.BlockSpec((tm, MEM_WIDTH), row),
                  pl.BlockSpec((tm, D_MODEL), row),
                  pl.BlockSpec((FOX_WIDTH, tn), colb(0)),
                  pl.BlockSpec((CONV_WIDTH, tn), colb(0)),
                  pl.BlockSpec((MEM_WIDTH, tn), colb(0)),
                  pl.BlockSpec((D_MODEL, tn), colb(0)),
                  pl.BlockSpec((D_MODEL, tn), colb(1)),
                  pl.BlockSpec((D_MODEL, tn), colb(2)),
                  pl.BlockSpec((1, tn), colb(0)),
                  pl.BlockSpec((1, tn), colb(1)),
                  pl.BlockSpec((1, tn), colb(2))],
        out_specs=pl.BlockSpec((tm, tn), lambda j, i: (i, j)),
        out_shape=jax.ShapeDtypeStruct((m, D_MODEL), BF16),
        compiler_params=_params("parallel", "parallel"),
        name="merge",
    )(a_fox, a_conv, a_mem, x16, wfo, wco, wmo, wmerge, wmerge, wmerge, bmerge, bmerge, bmerge)


def _out_kernel(m_ref, w_ref, x_ref, g_ref, b_ref, o_ref, *, alpha):
    half = m_ref.shape[0] // 2
    hs = [jnp.dot(m_ref[i * half:(i + 1) * half, :], w_ref[...], preferred_element_type=F32)
          for i in range(2)]
    for i, h in enumerate(hs):
        rows = slice(i * half, (i + 1) * half)
        r = alpha * x_ref[rows, :] + h
        mu = jnp.mean(r, axis=-1, keepdims=True)
        d = r - mu
        var = jnp.mean(d * d, axis=-1, keepdims=True)
        o_ref[rows, :] = d * lax.rsqrt(var + LN_EPS) * g_ref[...] + b_ref[...]


def _out_proj_norm(m16, wo16, x32, ln_g, ln_b, alpha, tm):
    m = x32.shape[0]
    return pl.pallas_call(
        functools.partial(_out_kernel, alpha=alpha),
        grid=(m // tm,),
        in_specs=[pl.BlockSpec((tm, D_MODEL), lambda i: (i, 0)),
                  pl.BlockSpec((D_MODEL, D_MODEL), lambda i: (0, 0)),
                  pl.BlockSpec((tm, D_MODEL), lambda i: (i, 0)),
                  pl.BlockSpec((1, D_MODEL), lambda i: (0, 0)),
                  pl.BlockSpec((1, D_MODEL), lambda i: (0, 0))],
        out_specs=pl.BlockSpec((tm, D_MODEL), lambda i: (i, 0)),
        out_shape=jax.ShapeDtypeStruct((m, D_MODEL), F32),
        compiler_params=_params("parallel"),
        name="out_proj_norm",
    )(m16, wo16, x32, ln_g, ln_b)


def _layer(x32, batch, wts, alpha, mkv16, prev_conv, k_past, v_past, past_logf, tiles):
    tm = tiles["tm"]
    wa = wts["w_qkv"]
    x16, zf = _cast_and_gate_logits(x32, wts["w_f"], tm)
    logf, c3 = _fox_prep(zf, wts["b_f"], past_logf, batch)
    if k_past is None:
        a_fox, k32, v32 = _fox_branch(x16, wa, wts["w_gfox"], c3, wts["sel"], wts["sel_ones"], batch,
                                      tiles["tq"])
    else:
        k32 = _project_heads(x16, wa, FOX_WIDTH, batch, tm, "proj_k")
        v32 = _project_heads(x16, wa, 2 * FOX_WIDTH, batch, tm, "proj_v")
        q16 = _project(x16, wa, 0, FOX_WIDTH, FOX_HEAD_DIM ** -0.5 * LOG2E, BF16, tm, FOX_WIDTH, "proj_q")
        g_fox = _project(x16, wts["w_gfox"], 0, FOX_WIDTH, 1.0, F32, tm, FOX_WIDTH, "proj_gfox")
        a_fox = _fox_attn_hist(q16, c3, wts["sel"], wts["sel_ones"], k32, v32, g_fox, k_past, v_past)
    a_conv, tail = _conv_branch(x16, wts["w_conv"], prev_conv, wts["conv_w"], wts["conv_b"],
                                batch, tiles["conv_nb"], tiles["conv_tc"])
    a_mem = _mem_branch(x16, wts["w_mem"], mkv16, batch, tiles["mem_tm"])
    m16 = _merge(a_fox, a_conv, a_mem, x16, wts["w_fox_out"], wts["w_conv_out"], wts["w_mem_out"],
                 wts["w_merge"], wts["b_merge"], tiles["merge_tm"], 512)
    y = _out_proj_norm(m16, wts["w_o"], x32, wts["ln_g"], wts["ln_b"], alpha, tiles["out_tm"])
    return y, k32, v32, logf, tail


def _layer_weights(l, w_in, fox_bf, conv_w, conv_b, w_mem_kv, w_fox_out, w_conv_out, w_mem_out,
                   w_merge, b_merge, w_o, ln_g, ln_b):
    wt = w_in[l].T.astype(BF16)
    part = lambda i, j: wt[IN_OFFS[i]:IN_OFFS[j]]
    sel, sel_ones = _bias_selectors()
    return {
        "w_qkv": part(0, 3),
        "w_gfox": part(4, 5),
        "w_conv": part(5, 9),
        "w_mem": part(9, 11),
        "w_f": jnp.pad(part(3, 4), ((0, LANES - FOX_HEADS), (0, 0))),
        "b_f": jnp.pad(fox_bf[l].astype(F32), (0, LANES - FOX_HEADS)).reshape(1, LANES),
        "sel": sel,
        "sel_ones": sel_ones,
        "conv_w": conv_w[l],
        "conv_b": conv_b[l].reshape(1, CONV_WIDTH),
        "w_mem_kv": w_mem_kv[l].astype(BF16),
        "w_fox_out": w_fox_out[l].astype(BF16),
        "w_conv_out": w_conv_out[l].astype(BF16),
        "w_mem_out": w_mem_out[l].astype(BF16),
        "w_merge": w_merge[l].astype(BF16),
        "b_merge": b_merge[l].reshape(1, N_BRANCH * D_MODEL),
        "w_o": w_o[l].astype(BF16),
        "ln_g": ln_g[l].reshape(1, D_MODEL),
        "ln_b": ln_b[l].reshape(1, D_MODEL),
    }


def _to_time_major(x):
    return jnp.transpose(x, (0, 2, 1, 3))


def kernel(x_prompt, x_sample, mem_prompt, cache_fox_k, cache_fox_v, cache_fox_logf, state_conv,
           cache_mem_k, cache_mem_v, w_in, fox_bf, conv_w, conv_b, w_mem_kv, w_fox_out, w_conv_out,
           w_mem_out, w_merge, b_merge, w_o, ln_g, ln_b):
    depth = w_in.shape[0]
    alpha = (2 * depth) ** 0.25
    bp, sp, _ = x_prompt.shape
    bs, ts, _ = x_sample.shape
    prompt_tiles = dict(tm=1024, kv_tm=512, tq=256, conv_nb=1, conv_tc=256, mem_tm=1024, merge_tm=512,
                        out_tm=512)
    sample_tiles = dict(tm=bs * ts, kv_tm=bs * ts, tq=ts, conv_nb=bs, conv_tc=256, mem_tm=bs * ts,
                        merge_tm=bs * ts, out_tm=bs * ts)

    hp = x_prompt.reshape(bp * sp, D_MODEL)
    hs = x_sample.reshape(bs * ts, D_MODEL)
    mem16 = mem_prompt.reshape(bp * MEM_TOKENS, D_MODEL).astype(BF16)
    outs = [[] for _ in range(10)]
    for l in range(depth):
        wts = _layer_weights(l, w_in, fox_bf, conv_w, conv_b, w_mem_kv, w_fox_out, w_conv_out,
                             w_mem_out, w_merge, b_merge, w_o, ln_g, ln_b)
        mkv32, mkv16 = _mem_kv(mem16, wts["w_mem_kv"], bp, 4)
        prev0 = jnp.zeros((bp, CONV_K - 1, CONV_WIDTH), F32)
        hp, k_p, v_p, lf_p, tail_p = _layer(hp, bp, wts, alpha, mkv16, prev0, None, None, None,
                                            prompt_tiles)
        ck = jnp.transpose(cache_fox_k[l], (0, 2, 1, 3))
        cv = jnp.transpose(cache_fox_v[l], (0, 2, 1, 3))
        clf = jnp.pad(cache_fox_logf[l].astype(F32), ((0, 0), (0, 0), (0, LANES - FOX_HEADS)))
        cmkv16 = jnp.concatenate([cache_mem_k[l].reshape(bs * MEM_TOKENS, MEM_WIDTH),
                                  cache_mem_v[l].reshape(bs * MEM_TOKENS, MEM_WIDTH)], axis=1).astype(BF16)
        hs, k_s, v_s, lf_s, tail_s = _layer(hs, bs, wts, alpha, cmkv16, state_conv[l].astype(F32),
                                            ck, cv, clf, sample_tiles)
        vals = (_to_time_major(k_p), _to_time_major(v_p),
                lf_p[:, :FOX_HEADS].reshape(bp, sp, FOX_HEADS), tail_p,
                _to_time_major(mkv32[0]), _to_time_major(mkv32[1]),
                _to_time_major(k_s), _to_time_major(v_s),
                lf_s[:, :FOX_HEADS].reshape(bs, ts, FOX_HEADS), tail_s)
        for acc, val in zip(outs, vals):
            acc.append(val)
    return (hp.reshape(bp, sp, D_MODEL), hs.reshape(bs, ts, D_MODEL)) + tuple(jnp.stack(o) for o in outs)
```
